```python
import jax, jax.numpy as jnp
from jax import lax
import numpy as np

D_MODEL = 1024
BATCH = 8
SEQ = 4096
DEPTH = 2

GRID_W = 64
CTX_LEN = 256
N_MIXERS = 2
D_FF = 4 * D_MODEL
CONV_W = 3
ML_HEADS = 8
ML_DV = D_MODEL // ML_HEADS
ML_DQK = ML_DV // 2
ML_QK_COLS = 2 * ML_HEADS * ML_DQK
ML_QKV_COLS = ML_QK_COLS + ML_HEADS * ML_DV
ML_PROJ_COLS = ML_QKV_COLS + ML_HEADS * ML_DV
ML_CHUNK = 128
EPS = 1e-6
N_CONV_LAYERS = (DEPTH + N_MIXERS - 1) // N_MIXERS
N_MLSTM_LAYERS = DEPTH // N_MIXERS

kernel_name = 'hybrid_shortconv_mlstm_dit'


def _rmsnorm(x, w):
    x32 = x.astype(jnp.float32)
    y = x32 * lax.rsqrt(jnp.mean(x32 * x32, axis=-1, keepdims=True) + EPS)
    return y.astype(x.dtype) * w


def _modulate(h, shift, scale):
    return h * (1 + scale) + shift


def _dwconv_centred(u, w, axis):
    n = u.shape[axis]
    half = CONV_W // 2
    pad = [(0, 0)] * u.ndim
    pad[axis] = (half, CONV_W - 1 - half)
    up = jnp.pad(u, pad)
    out = w[0] * lax.slice_in_dim(up, 0, n, axis=axis)
    for j in range(1, CONV_W):
        out = out + w[j] * lax.slice_in_dim(up, j, j + n, axis=axis)
    return out


def _short_conv_mixer(h, w_in, conv_w, w_out, rows):
    bsz, t_len, d = h.shape
    b_gate, c_gate, u = jnp.split(h @ w_in, 3, axis=-1)
    u = c_gate * u
    if rows is None:
        y = _dwconv_centred(u, conv_w, axis=1)
    else:
        y = _dwconv_centred(u.reshape(bsz, rows, GRID_W, d), conv_w, axis=2).reshape(bsz, t_len, d)
    return (b_gate * y) @ w_out


def _sqrelu_mlp(h, w1, w2):
    return jnp.square(jax.nn.relu(h @ w1)) @ w2


def _mlstm_inputs(h, w_qkvo, w_if, b_if, with_o):
    bsz, t_len, _ = h.shape
    p = h @ (w_qkvo if with_o else w_qkvo[:, :ML_QKV_COLS])
    q = p[..., :ML_HEADS * ML_DQK]
    k = p[..., ML_HEADS * ML_DQK:ML_QK_COLS]
    v = p[..., ML_QK_COLS:ML_QKV_COLS]
    o = p[..., ML_QKV_COLS:] if with_o else None

    def heads(a, dh):
        return a.reshape(bsz, t_len, ML_HEADS, dh).transpose(0, 2, 1, 3).astype(jnp.float32)

    q = heads(q, ML_DQK) * (ML_DQK ** -0.5)
    k = heads(k, ML_DQK)
    v = heads(v, ML_DV)
    g = (jnp.einsum('btd,rdg->rbgt', h, w_if) + b_if[:, None, :, None]).astype(jnp.float32)
    ig = g[:, :, :ML_HEADS]
    lf = jax.nn.log_sigmoid(g[:, :, ML_HEADS:])
    return q, k, v, o, ig, lf


def _mlstm_scan(q, k, v, ig, lf, state, with_output):
    bsz, nh, t_len, _ = q.shape
    dv = v.shape[-1]
    nc = t_len // ML_CHUNK

    def chunks(a):
        a = a.reshape(a.shape[:2] + (nc, ML_CHUNK) + a.shape[3:])
        return jnp.moveaxis(a, 2, 0)

    xs = (chunks(q), chunks(k), chunks(v), chunks(ig), chunks(lf))
    tri = jnp.tril(jnp.ones((ML_CHUNK, ML_CHUNK), dtype=bool))

    def body(carry, inp):
        c_mat, n_vec, m = carry
        qc, kc, vc, ic, fc = inp
        b = jnp.cumsum(fc, axis=-1)
        out = None
        if with_output:
            logd = b[..., :, None] - b[..., None, :] + ic[..., None, :]
            logd = jnp.where(tri, logd, -jnp.inf)
            inter = b + m[..., None]
            m_t = jnp.maximum(inter, jnp.max(logd, axis=-1))
            s = jnp.einsum('bhtd,bhsd->bhts', qc, kc) * jnp.exp(logd - m_t[..., None])
            w_inter = jnp.exp(inter - m_t)
            num = jnp.einsum('bhts,bhsv->bhtv', s, vc) + w_inter[..., None] * jnp.einsum('bhtd,bhdv->bhtv', qc, c_mat)
            den = jnp.sum(s, axis=-1) + w_inter * jnp.einsum('bhtd,bhd->bht', qc, n_vec)
            out = num / jnp.maximum(jnp.abs(den), jnp.exp(-m_t))[..., None]
        b_last = b[..., -1]
        dec = b_last[..., None] - b + ic
        m_new = jnp.maximum(b_last + m, jnp.max(dec, axis=-1))
        wk = jnp.exp(dec - m_new[..., None])
        a = jnp.exp(b_last + m - m_new)
        c_new = a[..., None, None] * c_mat + jnp.einsum('bhsd,bhsv->bhdv', kc * wk[..., None], vc)
        n_new = a[..., None] * n_vec + jnp.einsum('bhs,bhsd->bhd', wk, kc)
        return (c_new, n_new, m_new), out

    state, hs = lax.scan(body, state, xs)
    if with_output:
        hs = jnp.moveaxis(hs, 0, 2).reshape(bsz, nh, t_len, dv)
    return hs, state


def _mlstm_readout(h, o, norm_w, w_out):
    bsz, nh, t_len, dv = h.shape
    h = h.transpose(0, 2, 1, 3)
    h = h * lax.rsqrt(jnp.mean(h * h, axis=-1, keepdims=True) + EPS)
    h = h.reshape(bsz, t_len, nh * dv).astype(o.dtype) * norm_w
    return (jax.nn.sigmoid(o) * h) @ w_out


def _mlstm_mixer(hx, hc, w_qkvo, w_if, b_if, norm_w, w_out, ctx_out):
    qx, kx, vx, ox, igx, lfx = _mlstm_inputs(hx, w_qkvo, w_if, b_if, True)
    qc, kc, vc, oc, igc, lfc = _mlstm_inputs(hc, w_qkvo, w_if, b_if, ctx_out)
    bsz = hx.shape[0]
    zero = (jnp.zeros((bsz, ML_HEADS, ML_DQK, ML_DV), jnp.float32),
            jnp.zeros((bsz, ML_HEADS, ML_DQK), jnp.float32),
            jnp.zeros((bsz, ML_HEADS), jnp.float32))
    hx_sum = 0.0
    hc_sum = 0.0
    for r in range(2):
        if r == 0:
            flip = lambda a: a
        else:
            flip = lambda a: jnp.flip(a, axis=2)
        hc_r, st = _mlstm_scan(flip(qc), flip(kc), flip(vc), flip(igc[r]), flip(lfc[r]), zero, ctx_out)
        hx_r, _ = _mlstm_scan(flip(qx), flip(kx), flip(vx), flip(igx[r]), flip(lfx[r]), st, True)
        hx_sum = hx_sum + flip(hx_r)
        if ctx_out:
            hc_sum = hc_sum + flip(hc_r)
    yx = _mlstm_readout(hx_sum, ox, norm_w, w_out)
    yc = _mlstm_readout(hc_sum, oc, norm_w, w_out) if ctx_out else None
    return yx, yc


def setup_inputs(seed: int = 0) -> dict:
    key = jax.random.key(seed)
    ks = jax.random.split(key, 20)
    f32 = jnp.float32

    def nrm(k, shape, fan_in):
        return jax.random.normal(k, shape, f32) * (fan_in ** -0.5)

    x = jax.random.normal(ks[0], (BATCH, SEQ, D_MODEL), f32)
    c = jax.random.normal(ks[1], (BATCH, D_MODEL), f32)
    ctx = jax.random.normal(ks[2], (BATCH, CTX_LEN, D_MODEL), f32)
    c_ctx = jax.random.normal(ks[3], (D_MODEL,), f32)
    norm_w = 1.0 + 0.02 * jax.random.normal(ks[4], (DEPTH, 4, D_MODEL), f32)
    mod_w = nrm(ks[5], (DEPTH, D_MODEL, 6 * D_MODEL), D_MODEL)
    mod_b = 0.01 * jax.random.normal(ks[6], (DEPTH, 6 * D_MODEL), f32)
    mlp_w1 = nrm(ks[7], (DEPTH, D_MODEL, D_FF), D_MODEL)
    mlp_w2 = nrm(ks[8], (DEPTH, D_FF, D_MODEL), D_FF)
    conv_w_in = nrm(ks[9], (N_CONV_LAYERS, D_MODEL, 3 * D_MODEL), D_MODEL)
    conv_w = nrm(ks[10], (N_CONV_LAYERS, CONV_W, D_MODEL), CONV_W)
    conv_w_out = nrm(ks[11], (N_CONV_LAYERS, D_MODEL, D_MODEL), D_MODEL)
    ml_w_qkvo = nrm(ks[12], (N_MLSTM_LAYERS, D_MODEL, ML_PROJ_COLS), D_MODEL)
    ml_w_if = nrm(ks[13], (N_MLSTM_LAYERS, 2, D_MODEL, 2 * ML_HEADS), D_MODEL)
    ig_b = 0.1 * jax.random.normal(ks[14], (N_MLSTM_LAYERS, 2, ML_HEADS), f32)
    fg_b = jnp.linspace(3.0, 6.0, ML_HEADS, dtype=f32) + 0.1 * jax.random.normal(ks[15], (N_MLSTM_LAYERS, 2, ML_HEADS), f32)
    ml_b_if = jnp.concatenate([ig_b, fg_b], axis=-1)
    ml_norm_w = 1.0 + 0.02 * jax.random.normal(ks[16], (N_MLSTM_LAYERS, ML_HEADS * ML_DV), f32)
    ml_w_out = nrm(ks[17], (N_MLSTM_LAYERS, ML_HEADS * ML_DV, D_MODEL), ML_HEADS * ML_DV)
    return {'x': x, 'c': c, 'ctx': ctx, 'c_ctx': c_ctx, 'norm_w': norm_w, 'mod_w': mod_w, 'mod_b': mod_b,
            'mlp_w1': mlp_w1, 'mlp_w2': mlp_w2, 'conv_w_in': conv_w_in, 'conv_w': conv_w,
            'conv_w_out': conv_w_out, 'ml_w_qkvo': ml_w_qkvo, 'ml_w_if': ml_w_if, 'ml_b_if': ml_b_if,
            'ml_norm_w': ml_norm_w, 'ml_w_out': ml_w_out}


def reference(x, c, ctx, c_ctx, norm_w, mod_w, mod_b, mlp_w1, mlp_w2, conv_w_in, conv_w, conv_w_out,
              ml_w_qkvo, ml_w_if, ml_b_if, ml_norm_w, ml_w_out):
    rows = x.shape[1] // GRID_W
    silu_c = jax.nn.silu(c)[:, None, :]
    silu_cc = jax.nn.silu(c_ctx)
    cx = ctx
    for i in range(DEPTH):
        last = i == DEPTH - 1
        kind = i % N_MIXERS
        j = i // N_MIXERS
        nw = norm_w[i]
        mx = jnp.split(silu_c @ mod_w[i] + mod_b[i], 6, axis=-1)
        hx = _modulate(_rmsnorm(x, nw[0]), mx[0], mx[1])
        need_ctx = (not last) or kind == 1
        if need_ctx:
            mc = jnp.split(silu_cc @ mod_w[i] + mod_b[i], 6, axis=-1)
            hc = _modulate(_rmsnorm(cx, nw[0]), mc[0], mc[1])
        if kind == 0:
            yx = _short_conv_mixer(hx, conv_w_in[j], conv_w[j], conv_w_out[j], rows)
            yc = _short_conv_mixer(hc, conv_w_in[j], conv_w[j], conv_w_out[j], None) if not last else None
        else:
            yx, yc = _mlstm_mixer(hx, hc, ml_w_qkvo[j], ml_w_if[j], ml_b_if[j], ml_norm_w[j], ml_w_out[j],
                                  not last)
        x = x + mx[2] * _rmsnorm(yx, nw[1])
        x = x + mx[5] * _rmsnorm(_sqrelu_mlp(_modulate(_rmsnorm(x, nw[2]), mx[3], mx[4]), mlp_w1[i], mlp_w2[i]), nw[3])
        if not last:
            cx = cx + mc[2] * _rmsnorm(yc, nw[1])
            cx = cx + mc[5] * _rmsnorm(_sqrelu_mlp(_modulate(_rmsnorm(cx, nw[2]), mc[3], mc[4]), mlp_w1[i], mlp_w2[i]), nw[3])
    return x
```

```python
import functools

import jax
import jax.numpy as jnp
from jax import lax
from jax.experimental import pallas as pl
from jax.experimental.pallas import tpu as pltpu

EPS = 1e-6
GRID_W = 64
ML_HEADS = 8
ML_DV = 128
ML_DQK = 64
ML_CHUNK = 128
GATE_LANES = 128
NEG_BIG = -1e30

VMEM_LIMIT_BYTES = 56 * 1024 * 1024
TOKEN_TILE = 512
SCAN_BLOCK = 512
FF_CHUNK = 1024
CONV_COL_CHUNK = 512
ADALN_COL_TILE = 1536

f32 = jnp.float32
bf16 = jnp.bfloat16


def _dot(a, b):
    return jnp.dot(a, b, preferred_element_type=f32)


def _rms(x, w):
    return x * lax.rsqrt(jnp.mean(x * x, axis=-1, keepdims=True) + EPS) * w


def _resident(shape):
    nd = len(shape)
    return pl.BlockSpec(shape, lambda *_: (0,) * nd, pipeline_mode=pl.Buffered(1))


def _params(*sem):
    return pltpu.CompilerParams(dimension_semantics=sem, vmem_limit_bytes=VMEM_LIMIT_BYTES)


def _adaln_kernel(c_ref, w_ref, b_ref, o_ref):
    c = c_ref[...]
    s = c * jax.nn.sigmoid(c)
    w = w_ref[0]
    sh = s.astype(bf16)
    sl = (s - sh.astype(f32)).astype(bf16)
    wh = w.astype(bf16)
    wl = (w - wh.astype(f32)).astype(bf16)
    o_ref[0] = _dot(sh, wh) + _dot(sh, wl) + _dot(sl, wh) + b_ref[0]


def _adaln(cc, mod_w, mod_b):
    depth, d, n = mod_w.shape
    rows = cc.shape[0]
    tn = ADALN_COL_TILE
    return pl.pallas_call(
        _adaln_kernel,
        grid=(depth, n // tn),
        in_specs=[pl.BlockSpec((rows, d), lambda i, j: (0, 0)),
                  pl.BlockSpec((1, d, tn), lambda i, j: (i, 0, j)),
                  pl.BlockSpec((1, 1, tn), lambda i, j: (i, 0, j))],
        out_specs=pl.BlockSpec((1, rows, tn), lambda i, j: (i, 0, j)),
        out_shape=jax.ShapeDtypeStruct((depth, rows, n), f32),
        compiler_params=_params("arbitrary", "arbitrary"),
        name="adaln",
    )(cc, mod_w, mod_b.reshape(depth, 1, n))


def _conv_mixer_kernel(x_ref, mod_ref, nw_ref, win_ref, cw_ref, wout_ref, o_ref, z_ref, *, row_len):
    x = x_ref[...]
    mod = mod_ref[0]
    nw = nw_ref[...]
    tm, d = x.shape
    h = (_rms(x, nw[0:1]) * (1.0 + mod[1:2]) + mod[0:1]).astype(bf16)
    cw = cw_ref[...]
    cc = CONV_COL_CHUNK
    pos = lax.broadcasted_iota(jnp.int32, (tm, cc), 0) % row_len
    for j in range(d // cc):
        cs = slice(j * cc, (j + 1) * cc)
        b_gate = _dot(h, win_ref[:, j * cc:(j + 1) * cc])
        c_gate = _dot(h, win_ref[:, d + j * cc:d + (j + 1) * cc])
        u = c_gate * _dot(h, win_ref[:, 2 * d + j * cc:2 * d + (j + 1) * cc])
        u_prev = jnp.where(pos == 0, 0.0, pltpu.roll(u, 1, 0))
        u_next = jnp.where(pos == row_len - 1, 0.0, pltpu.roll(u, tm - 1, 0))
        y = cw[0:1, cs] * u_prev + cw[1:2, cs] * u + cw[2:3, cs] * u_next
        z_ref[:, cs] = (b_gate * y).astype(bf16)
    yx = _dot(z_ref[...], wout_ref[...])
    o_ref[...] = x + mod[2:3] * _rms(yx, nw[1:2])


def _conv_mixer(x2, mod, nw, w_in, conv_w, w_out, *, row_len, tokens_per_mod):
    n, d = x2.shape
    tm = TOKEN_TILE
    assert n % tm == 0 and tm % row_len == 0 and tokens_per_mod % tm == 0
    return pl.pallas_call(
        functools.partial(_conv_mixer_kernel, row_len=row_len),
        grid=(n // tm,),
        in_specs=[pl.BlockSpec((tm, d), lambda i: (i, 0)),
                  pl.BlockSpec((1, 6, d), lambda i: (i * tm // tokens_per_mod, 0, 0)),
                  _resident(nw.shape), _resident(w_in.shape), _resident(conv_w.shape),
                  _resident(w_out.shape)],
        out_specs=pl.BlockSpec((tm, d), lambda i: (i, 0)),
        out_shape=jax.ShapeDtypeStruct((n, d), f32),
        scratch_shapes=[pltpu.VMEM((tm, d), bf16)],
        compiler_params=_params("parallel"),
        name="conv_mixer",
    )(x2, mod, nw, w_in, conv_w, w_out)


def _mlp_kernel(x_ref, mod_ref, nw_ref, w1_ref, w2_ref, o_ref, acc_ref):
    x = x_ref[...]
    mod = mod_ref[0]
    nw = nw_ref[...]
    h = (_rms(x, nw[2:3]) * (1.0 + mod[4:5]) + mod[3:4]).astype(bf16)
    ff = w1_ref.shape[1]
    for j in range(ff // FF_CHUNK):
        a = jnp.maximum(_dot(h, w1_ref[:, j * FF_CHUNK:(j + 1) * FF_CHUNK]), 0.0)
        part = _dot((a * a).astype(bf16), w2_ref[j * FF_CHUNK:(j + 1) * FF_CHUNK, :])
        if j == 0:
            acc_ref[...] = part
        else:
            acc_ref[...] += part
    o_ref[...] = x + mod[5:6] * _rms(acc_ref[...], nw[3:4])


def _mlp(x2, mod, nw, w1, w2, *, tokens_per_mod):
    n, d = x2.shape
    tm = TOKEN_TILE
    assert n % tm == 0 and tokens_per_mod % tm == 0
    return pl.pallas_call(
        _mlp_kernel,
        grid=(n // tm,),
        in_specs=[pl.BlockSpec((tm, d), lambda i: (i, 0)),
                  pl.BlockSpec((1, 6, d), lambda i: (i * tm // tokens_per_mod, 0, 0)),
                  _resident(nw.shape), _resident(w1.shape), _resident(w2.shape)],
        out_specs=pl.BlockSpec((tm, d), lambda i: (i, 0)),
        out_shape=jax.ShapeDtypeStruct((n, d), f32),
        scratch_shapes=[pltpu.VMEM((tm, d), f32)],
        compiler_params=_params("parallel"),
        name="sqrelu_mlp",
    )(x2, mod, nw, w1, w2)


def _log_sigmoid(x):
    return jnp.minimum(x, 0.0) - jnp.log1p(jnp.exp(-jnp.abs(x)))


def _ml_proj_kernel(x_ref, mod_ref, nw_ref, wp_ref, wg_ref, bg_ref,
                    q_ref, k_ref, v_ref, o_ref, ac_ref, bc_ref, ar_ref):
    x = x_ref[...]
    mod = mod_ref[0]
    nw = nw_ref[...]
    tm = x.shape[0]
    h = (_rms(x, nw[0:1]) * (1.0 + mod[1:2]) + mod[0:1]).astype(bf16)
    nqk = ML_HEADS * ML_DQK
    nv = ML_HEADS * ML_DV
    q_ref[...] = (_dot(h, wp_ref[:, 0:nqk]) * (ML_DQK ** -0.5)).astype(bf16)
    k_ref[...] = _dot(h, wp_ref[:, nqk:2 * nqk]).astype(bf16)
    v_ref[...] = _dot(h, wp_ref[:, 2 * nqk:2 * nqk + nv]).astype(bf16)
    o_ref[...] = _dot(h, wp_ref[:, 2 * nqk + nv:2 * nqk + 2 * nv]).astype(bf16)

    g = _dot(h, wg_ref[...]) + bg_ref[...]
    ig = g[:, :GATE_LANES]
    lf = _log_sigmoid(g[:, GATE_LANES:])
    pos = lax.broadcasted_iota(jnp.int32, (tm, GATE_LANES), 0) % ML_CHUNK
    lane = lax.broadcasted_iota(jnp.int32, (tm, GATE_LANES), 1)
    fwd = lf
    bwd = lf
    step = 1
    while step < ML_CHUNK:
        fwd = fwd + jnp.where(pos >= step, pltpu.roll(fwd, step, 0), 0.0)
        bwd = bwd + jnp.where(pos < ML_CHUNK - step, pltpu.roll(bwd, tm - step, 0), 0.0)
        step *= 2
    b = jnp.where(lane < ML_HEADS, fwd, bwd)
    a = ig - b
    ac_ref[...] = a
    bc_ref[...] = b
    a_t = a.T
    for ch in range(tm // ML_CHUNK):
        ar_ref[ch] = a_t[0:2 * ML_HEADS, ch * ML_CHUNK:(ch + 1) * ML_CHUNK]


def _ml_proj(x2, mod, nw, w_proj, w_gate, b_gate, *, tokens_per_mod):
    n, d = x2.shape
    tm = TOKEN_TILE
    assert n % tm == 0 and tokens_per_mod % tm == 0 and tm % ML_CHUNK == 0
    nqk = ML_HEADS * ML_DQK
    nv = ML_HEADS * ML_DV
    tok = lambda width: pl.BlockSpec((tm, width), lambda i: (i, 0))
    return pl.pallas_call(
        _ml_proj_kernel,
        grid=(n // tm,),
        in_specs=[tok(d),
                  pl.BlockSpec((1, 6, d), lambda i: (i * tm // tokens_per_mod, 0, 0)),
                  _resident(nw.shape), _resident(w_proj.shape), _resident(w_gate.shape),
                  _resident(b_gate.shape)],
        out_specs=[tok(nqk), tok(nqk), tok(nv), tok(nv), tok(GATE_LANES), tok(GATE_LANES),
                   pl.BlockSpec((tm // ML_CHUNK, 2 * ML_HEADS, ML_CHUNK), lambda i: (i, 0, 0))],
        out_shape=[jax.ShapeDtypeStruct((n, nqk), bf16), jax.ShapeDtypeStruct((n, nqk), bf16),
                   jax.ShapeDtypeStruct((n, nv), bf16), jax.ShapeDtypeStruct((n, nv), bf16),
                   jax.ShapeDtypeStruct((n, GATE_LANES), f32), jax.ShapeDtypeStruct((n, GATE_LANES), f32),
                   jax.ShapeDtypeStruct((n // ML_CHUNK, 2 * ML_HEADS, ML_CHUNK), f32)],
        compiler_params=_params("parallel"),
        name="mlstm_proj",
    )(x2, mod, nw, w_proj, w_gate, b_gate)


def _ml_chunk(q_ref, k_ref, v_ref, ac_ref, bc_ref, ar_ref, o_ref, c_ref, n_ref, m_ref, ch, reverse):
    L = ML_CHUNK
    rows = slice(ch * L, (ch + 1) * L)
    t_idx = lax.broadcasted_iota(jnp.int32, (L, L), 0)
    s_idx = lax.broadcasted_iota(jnp.int32, (L, L), 1)
    mask = (s_idx >= t_idx) if reverse else (s_idx <= t_idx)
    last = 0 if reverse else L - 1
    for hd in range(ML_HEADS):
        g = (ML_HEADS if reverse else 0) + hd
        qs = slice(hd * ML_DQK, (hd + 1) * ML_DQK)
        vs = slice(hd * ML_DV, (hd + 1) * ML_DV)
        k_h = k_ref[rows, qs]
        v_h = v_ref[rows, vs]
        a_row = ar_ref[ch, g:g + 1, :]
        a_col = ac_ref[rows, g:g + 1]
        m_old = m_ref[hd][0:1, 0:1]
        c_old = c_ref[hd]
        n_old = n_ref[hd]
        m_last = jnp.maximum(m_old, jnp.max(a_row, axis=1, keepdims=True))
        if o_ref is not None:
            q_h = q_ref[rows, qs]
            b_col = bc_ref[rows, g:g + 1]
            masked = jnp.where(mask, a_row, NEG_BIG)
            m_run = jnp.maximum(m_old, jnp.max(masked, axis=1, keepdims=True))
            decay = jnp.exp(masked - m_run)
            s_mat = lax.dot_general(q_h, k_h, (((1,), (1,)), ((), ())), preferred_element_type=f32) * decay
            w_inter = jnp.exp(m_old - m_run)
            num = _dot(s_mat.astype(bf16), v_h) + w_inter * _dot(q_h, c_old.astype(bf16))
            qn = jnp.sum(q_h.astype(f32) * n_old, axis=1, keepdims=True)
            den = jnp.sum(s_mat, axis=1, keepdims=True) + w_inter * qn
            floor = jnp.exp(-(b_col + m_run))
            o_ref[rows, vs] = (num / jnp.maximum(jnp.abs(den), floor)).astype(o_ref.dtype)
        b_last = bc_ref[ch * L + last:ch * L + last + 1, g:g + 1]
        kw = k_h.astype(f32) * jnp.exp(a_col - m_last)
        scale = jnp.exp(m_old - m_last)
        c_ref[hd] = scale * c_old + lax.dot_general(
            kw.astype(bf16), v_h, (((0,), (0,)), ((), ())), preferred_element_type=f32)
        n_ref[hd] = scale * n_old + jnp.sum(kw, axis=0, keepdims=True)
        m_ref[hd] = jnp.broadcast_to(b_last + m_last, m_ref.shape[1:])


def _ml_scan_kernel(qc_ref, kc_ref, vc_ref, acc_ref, bcc_ref, arc_ref,
                    qx_ref, kx_ref, vx_ref, acx_ref, bcx_ref, arx_ref,
                    o_ref, c_ref, n_ref, m_ref, *, reverse):
    j = pl.program_id(1)

    def run(q_ref, k_ref, v_ref, ac_ref, bc_ref, ar_ref, out_ref):
        nch = k_ref.shape[0] // ML_CHUNK
        order = range(nch - 1, -1, -1) if reverse else range(nch)
        for ch in order:
            _ml_chunk(q_ref, k_ref, v_ref, ac_ref, bc_ref, ar_ref, out_ref, c_ref, n_ref, m_ref, ch, reverse)

    @pl.when(j == 0)
    def _():
        c_ref[...] = jnp.zeros_like(c_ref)
        n_ref[...] = jnp.zeros_like(n_ref)
        m_ref[...] = jnp.zeros_like(m_ref)
        run(qc_ref, kc_ref, vc_ref, acc_ref, bcc_ref, arc_ref, None)

    @pl.when(j > 0)
    def _():
        run(qx_ref, kx_ref, vx_ref, acx_ref, bcx_ref, arx_ref, o_ref)


def _ml_scan(ctx_parts, lat_parts, *, batch, reverse):
    qc, kc, vc, acc, bcc, arc = ctx_parts
    qx, kx, vx, acx, bcx, arx = lat_parts
    t_ctx = qc.shape[0] // batch
    t_lat = qx.shape[0] // batch
    tb = SCAN_BLOCK
    assert t_lat % tb == 0 and tb % ML_CHUNK == 0 and t_ctx % ML_CHUNK == 0
    nblk = t_lat // tb
    nqk = ML_HEADS * ML_DQK
    nv = ML_HEADS * ML_DV

    def lat_blk(b, j):
        step = jnp.maximum(j - 1, 0)
        return b * nblk + (nblk - 1 - step if reverse else step)

    ctx_tok = lambda width: pl.BlockSpec((t_ctx, width), lambda b, j: (b, 0))
    lat_tok = lambda width: pl.BlockSpec((tb, width), lambda b, j: (lat_blk(b, j), 0))
    return pl.pallas_call(
        functools.partial(_ml_scan_kernel, reverse=reverse),
        grid=(batch, 1 + nblk),
        in_specs=[ctx_tok(nqk), ctx_tok(nqk), ctx_tok(nv), ctx_tok(GATE_LANES), ctx_tok(GATE_LANES),
                  pl.BlockSpec((t_ctx // ML_CHUNK, 2 * ML_HEADS, ML_CHUNK), lambda b, j: (b, 0, 0)),
                  lat_tok(nqk), lat_tok(nqk), lat_tok(nv), lat_tok(GATE_LANES), lat_tok(GATE_LANES),
                  pl.BlockSpec((tb // ML_CHUNK, 2 * ML_HEADS, ML_CHUNK), lambda b, j: (lat_blk(b, j), 0, 0))],
        out_specs=lat_tok(nv),
        out_shape=jax.ShapeDtypeStruct((batch * t_lat, nv), bf16),
        scratch_shapes=[pltpu.VMEM((ML_HEADS, ML_DQK, ML_DV), f32),
                        pltpu.VMEM((ML_HEADS, 1, ML_DQK), f32),
                        pltpu.VMEM((ML_HEADS, 8, 128), f32)],
        compiler_params=_params("parallel", "arbitrary"),
        name="mlstm_scan_bwd" if reverse else "mlstm_scan_fwd",
    )(qc, kc, vc, acc, bcc, arc, qx, kx, vx, acx, bcx, arx)


def _ml_readout_kernel(hf_ref, hb_ref, og_ref, x_ref, mod_ref, nw_ref, mnw_ref, wout_ref, o_ref, z_ref):
    x = x_ref[...]
    mod = mod_ref[0]
    nw = nw_ref[...]
    mnw = mnw_ref[...]
    for hd in range(ML_HEADS):
        vs = slice(hd * ML_DV, (hd + 1) * ML_DV)
        hs = hf_ref[:, vs].astype(f32) + hb_ref[:, vs].astype(f32)
        hn = hs * lax.rsqrt(jnp.mean(hs * hs, axis=-1, keepdims=True) + EPS) * mnw[:, vs]
        z_ref[:, vs] = (jax.nn.sigmoid(og_ref[:, vs].astype(f32)) * hn).astype(bf16)
    yx = _dot(z_ref[...], wout_ref[...])
    o_ref[...] = x + mod[2:3] * _rms(yx, nw[1:2])


def _ml_readout(hf, hb, og, x2, mod, nw, ml_norm_w, w_out, *, tokens_per_mod):
    n, d = x2.shape
    tm = TOKEN_TILE
    nv = ML_HEADS * ML_DV
    assert n % tm == 0 and tokens_per_mod % tm == 0
    tok = lambda width: pl.BlockSpec((tm, width), lambda i: (i, 0))
    return pl.pallas_call(
        _ml_readout_kernel,
        grid=(n // tm,),
        in_specs=[tok(nv), tok(nv), tok(nv), tok(d),
                  pl.BlockSpec((1, 6, d), lambda i: (i * tm // tokens_per_mod, 0, 0)),
                  _resident(nw.shape), _resident(ml_norm_w.shape), _resident(w_out.shape)],
        out_specs=tok(d),
        out_shape=jax.ShapeDtypeStruct((n, d), f32),
        scratch_shapes=[pltpu.VMEM((tm, nv), bf16)],
        compiler_params=_params("parallel"),
        name="mlstm_readout",
    )(hf, hb, og, x2, mod, nw, ml_norm_w, w_out)


def kernel(x, c, ctx, c_ctx, norm_w, mod_w, mod_b, mlp_w1, mlp_w2, conv_w_in, conv_w, conv_w_out,
           ml_w_qkvo, ml_w_if, ml_b_if, ml_norm_w, ml_w_out):
    batch, seq, d = x.shape
    t_ctx = ctx.shape[1]
    depth = norm_w.shape[0]
    assert depth == 2 and seq % GRID_W == 0 and d == ML_HEADS * ML_DV

    pad_rows = -(batch + 1) % 8
    cc = jnp.concatenate([c, c_ctx[None, :], jnp.zeros((pad_rows, d), f32)], axis=0)
    mod = _adaln(cc, mod_w, mod_b).reshape(depth, batch + 1 + pad_rows, 6, d)

    xs = x.reshape(batch * seq, d)
    cs = ctx.reshape(batch * t_ctx, d)

    mx, mc = mod[0, :batch], mod[0, batch:batch + 1]
    w_in = conv_w_in[0].astype(bf16)
    w_out = conv_w_out[0].astype(bf16)
    w1 = mlp_w1[0].astype(bf16)
    w2 = mlp_w2[0].astype(bf16)
    xs = _conv_mixer(xs, mx, norm_w[0], w_in, conv_w[0], w_out, row_len=GRID_W, tokens_per_mod=seq)
    xs = _mlp(xs, mx, norm_w[0], w1, w2, tokens_per_mod=seq)
    cs = _conv_mixer(cs, mc, norm_w[0], w_in, conv_w[0], w_out, row_len=t_ctx, tokens_per_mod=batch * t_ctx)
    cs = _mlp(cs, mc, norm_w[0], w1, w2, tokens_per_mod=batch * t_ctx)

    mx, mc = mod[1, :batch], mod[1, batch:batch + 1]
    w_proj = ml_w_qkvo[0].astype(bf16)
    nh = ML_HEADS
    zpad = jnp.zeros((d, GATE_LANES - 2 * nh), f32)
    w_gate = jnp.concatenate([ml_w_if[0, 0, :, :nh], ml_w_if[0, 1, :, :nh], zpad,
                              ml_w_if[0, 0, :, nh:], ml_w_if[0, 1, :, nh:], zpad], axis=1).astype(bf16)
    bpad = jnp.zeros((GATE_LANES - 2 * nh,), f32)
    b_gate = jnp.concatenate([ml_b_if[0, 0, :nh], ml_b_if[0, 1, :nh], bpad,
                              ml_b_if[0, 0, nh:], ml_b_if[0, 1, nh:], bpad])[None, :]
    lat = _ml_proj(xs, mx, norm_w[1], w_proj, w_gate, b_gate, tokens_per_mod=seq)
    cxp = _ml_proj(cs, mc, norm_w[1], w_proj, w_gate, b_gate, tokens_per_mod=batch * t_ctx)
    lat_parts = (lat[0], lat[1], lat[2], lat[4], lat[5], lat[6])
    ctx_parts = (cxp[0], cxp[1], cxp[2], cxp[4], cxp[5], cxp[6])
    hf = _ml_scan(ctx_parts, lat_parts, batch=batch, reverse=False)
    hb = _ml_scan(ctx_parts, lat_parts, batch=batch, reverse=True)
    xs = _ml_readout(hf, hb, lat[3], xs, mx, norm_w[1], ml_norm_w[0][None, :], ml_w_out[0].astype(bf16),
                     tokens_per_mod=seq)
    xs = _mlp(xs, mx, norm_w[1], mlp_w1[1].astype(bf16), mlp_w2[1].astype(bf16), tokens_per_mod=seq)
    return xs.reshape(batch, seq, d)
```

```python
import functools

import jax
import jax.numpy as jnp
from jax import lax
from jax.experimental import pallas as pl
from jax.experimental.pallas import tpu as pltpu

EPS = 1e-6
GRID_W = 64
ML_HEADS = 8
ML_DV = 128
ML_DQK = 64
ML_CHUNK = 128
GATE_LANES = 128
N_GATE_ROWS = 5
STATE_ROWS = ML_DV + 16
NEG_BIG = -1e30

VMEM_LIMIT_BYTES = 56 * 1024 * 1024
TOKEN_TILE = 512
SCAN_BLOCK = 512
FF_CHUNK = 1024
CONV_COL_CHUNK = 512
ADALN_COL_TILE = 1536

f32 = jnp.float32
bf16 = jnp.bfloat16


def _dot(a, b):
    return jnp.dot(a, b, preferred_element_type=f32)


def _rms(x, w):
    return x * lax.rsqrt(jnp.mean(x * x, axis=-1, keepdims=True) + EPS) * w


def _resident(shape):
    nd = len(shape)
    return pl.BlockSpec(shape, lambda *_: (0,) * nd, pipeline_mode=pl.Buffered(1))


def _params(*sem):
    return pltpu.CompilerParams(dimension_semantics=sem, vmem_limit_bytes=VMEM_LIMIT_BYTES)


def _adaln_kernel(c_ref, w_ref, b_ref, o_ref):
    c = c_ref[...]
    s = c * jax.nn.sigmoid(c)
    w = w_ref[0]
    sh = s.astype(bf16)
    sl = (s - sh.astype(f32)).astype(bf16)
    wh = w.astype(bf16)
    wl = (w - wh.astype(f32)).astype(bf16)
    o_ref[0] = _dot(sh, wh) + _dot(sh, wl) + _dot(sl, wh) + b_ref[0]


def _adaln(cc, mod_w, mod_b):
    depth, d, n = mod_w.shape
    rows = cc.shape[0]
    tn = ADALN_COL_TILE
    return pl.pallas_call(
        _adaln_kernel,
        grid=(depth, n // tn),
        in_specs=[pl.BlockSpec((rows, d), lambda i, j: (0, 0)),
                  pl.BlockSpec((1, d, tn), lambda i, j: (i, 0, j)),
                  pl.BlockSpec((1, 1, tn), lambda i, j: (i, 0, j))],
        out_specs=pl.BlockSpec((1, rows, tn), lambda i, j: (i, 0, j)),
        out_shape=jax.ShapeDtypeStruct((depth, rows, n), f32),
        compiler_params=_params("arbitrary", "arbitrary"),
        name="adaln",
    )(cc, mod_w, mod_b.reshape(depth, 1, n))


def _conv_mixer_kernel(x_ref, mod_ref, nw_ref, win_ref, cw_ref, wout_ref, o_ref, z_ref, *, row_len):
    x = x_ref[...]
    mod = mod_ref[0]
    nw = nw_ref[...]
    tm, d = x.shape
    h = (_rms(x, nw[0:1]) * (1.0 + mod[1:2]) + mod[0:1]).astype(bf16)
    cw = cw_ref[...]
    cc = CONV_COL_CHUNK
    pos = lax.broadcasted_iota(jnp.int32, (tm, cc), 0) % row_len
    for j in range(d // cc):
        cs = slice(j * cc, (j + 1) * cc)
        b_gate = _dot(h, win_ref[:, j * cc:(j + 1) * cc])
        c_gate = _dot(h, win_ref[:, d + j * cc:d + (j + 1) * cc])
        u = c_gate * _dot(h, win_ref[:, 2 * d + j * cc:2 * d + (j + 1) * cc])
        u_prev = jnp.where(pos == 0, 0.0, pltpu.roll(u, 1, 0))
        u_next = jnp.where(pos == row_len - 1, 0.0, pltpu.roll(u, tm - 1, 0))
        y = cw[0:1, cs] * u_prev + cw[1:2, cs] * u + cw[2:3, cs] * u_next
        z_ref[:, cs] = (b_gate * y).astype(bf16)
    yx = _dot(z_ref[...], wout_ref[...])
    o_ref[...] = x + mod[2:3] * _rms(yx, nw[1:2])


def _conv_mixer(x2, mod, nw, w_in, conv_w, w_out, *, row_len, tokens_per_mod):
    n, d = x2.shape
    tm = TOKEN_TILE
    assert n % tm == 0 and tm % row_len == 0 and tokens_per_mod % tm == 0
    return pl.pallas_call(
        functools.partial(_conv_mixer_kernel, row_len=row_len),
        grid=(n // tm,),
        in_specs=[pl.BlockSpec((tm, d), lambda i: (i, 0)),
                  pl.BlockSpec((1, 6, d), lambda i: (i * tm // tokens_per_mod, 0, 0)),
                  _resident(nw.shape), _resident(w_in.shape), _resident(conv_w.shape),
                  _resident(w_out.shape)],
        out_specs=pl.BlockSpec((tm, d), lambda i: (i, 0)),
        out_shape=jax.ShapeDtypeStruct((n, d), f32),
        scratch_shapes=[pltpu.VMEM((tm, d), bf16)],
        compiler_params=_params("parallel"),
        name="conv_mixer",
    )(x2, mod, nw, w_in, conv_w, w_out)


def _mlp_kernel(x_ref, mod_ref, nw_ref, w1_ref, w2_ref, o_ref, acc_ref):
    x = x_ref[...]
    mod = mod_ref[0]
    nw = nw_ref[...]
    h = (_rms(x, nw[2:3]) * (1.0 + mod[4:5]) + mod[3:4]).astype(bf16)
    ff = w1_ref.shape[1]
    for j in range(ff // FF_CHUNK):
        a = jnp.maximum(_dot(h, w1_ref[:, j * FF_CHUNK:(j + 1) * FF_CHUNK]), 0.0)
        part = _dot((a * a).astype(bf16), w2_ref[j * FF_CHUNK:(j + 1) * FF_CHUNK, :])
        if j == 0:
            acc_ref[...] = part
        else:
            acc_ref[...] += part
    o_ref[...] = x + mod[5:6] * _rms(acc_ref[...], nw[3:4])


def _mlp(x2, mod, nw, w1, w2, *, tokens_per_mod):
    n, d = x2.shape
    tm = TOKEN_TILE
    assert n % tm == 0 and tokens_per_mod % tm == 0
    return pl.pallas_call(
        _mlp_kernel,
        grid=(n // tm,),
        in_specs=[pl.BlockSpec((tm, d), lambda i: (i, 0)),
                  pl.BlockSpec((1, 6, d), lambda i: (i * tm // tokens_per_mod, 0, 0)),
                  _resident(nw.shape), _resident(w1.shape), _resident(w2.shape)],
        out_specs=pl.BlockSpec((tm, d), lambda i: (i, 0)),
        out_shape=jax.ShapeDtypeStruct((n, d), f32),
        scratch_shapes=[pltpu.VMEM((tm, d), f32)],
        compiler_params=_params("parallel"),
        name="sqrelu_mlp",
    )(x2, mod, nw, w1, w2)


def _log_sigmoid(x):
    return jnp.minimum(x, 0.0) - jnp.log1p(jnp.exp(-jnp.abs(x)))


def _ml_proj_kernel(x_ref, mod_ref, nw_ref, wpt_ref, wp_ref, wg_ref, bg_ref,
                    qt_ref, k_ref, vt_ref, o_ref, ac_ref, gr_ref):
    x = x_ref[...]
    mod = mod_ref[0]
    nw = nw_ref[...]
    tm = x.shape[0]
    h32 = _rms(x, nw[0:1]) * (1.0 + mod[1:2]) + mod[0:1]
    h = h32.astype(bf16)
    h_t = h32.T.astype(bf16)
    nqk = ML_HEADS * ML_DQK
    qt_ref[...] = (_dot(wpt_ref[0:nqk, :], h_t) * (ML_DQK ** -0.5)).astype(bf16)
    vt_ref[...] = _dot(wpt_ref[nqk:, :], h_t).astype(bf16)
    k_ref[...] = _dot(h, wp_ref[:, 0:nqk]).astype(bf16)
    o_ref[...] = _dot(h, wp_ref[:, nqk:]).astype(bf16)

    g = _dot(h, wg_ref[...]) + bg_ref[...]
    ig = g[:, :GATE_LANES]
    lf = _log_sigmoid(g[:, GATE_LANES:])
    pos = lax.broadcasted_iota(jnp.int32, (tm, GATE_LANES), 0) % ML_CHUNK
    is_fwd = lax.broadcasted_iota(jnp.int32, (tm, GATE_LANES), 1) < ML_HEADS

    def chunk_scan(v, op, fill):
        pre, suf = v, v
        step = 1
        while step < ML_CHUNK:
            pre = op(pre, jnp.where(pos >= step, pltpu.roll(pre, step, 0), fill))
            suf = op(suf, jnp.where(pos < ML_CHUNK - step, pltpu.roll(suf, tm - step, 0), fill))
            step *= 2
        return pre, suf

    b_pre, b_suf = chunk_scan(lf, jnp.add, 0.0)
    b = jnp.where(is_fwd, b_pre, b_suf)
    tot = b_pre + b_suf - lf
    a = ig - b
    a_pre, a_suf = chunk_scan(a, jnp.maximum, NEG_BIG)
    cm = jnp.where(is_fwd, a_pre, a_suf)
    am = jnp.maximum(a_pre, a_suf)
    ac_ref[...] = a
    for i, arr in enumerate((a, cm, am, b, tot)):
        arr_t = arr.T
        for ch in range(tm // ML_CHUNK):
            gr_ref[ch, i] = arr_t[0:2 * ML_HEADS, ch * ML_CHUNK:(ch + 1) * ML_CHUNK]


def _ml_proj(x2, mod, nw, w_proj_t, w_proj, w_gate, b_gate, *, tokens_per_mod):
    n, d = x2.shape
    tm = TOKEN_TILE
    assert n % tm == 0 and tokens_per_mod % tm == 0 and tm % ML_CHUNK == 0
    nqk = ML_HEADS * ML_DQK
    nv = ML_HEADS * ML_DV
    tok = lambda width: pl.BlockSpec((tm, width), lambda i: (i, 0))
    tok_t = lambda width: pl.BlockSpec((width, tm), lambda i: (0, i))
    gr_shape = (n // ML_CHUNK, N_GATE_ROWS, 2 * ML_HEADS, ML_CHUNK)
    return pl.pallas_call(
        _ml_proj_kernel,
        grid=(n // tm,),
        in_specs=[tok(d),
                  pl.BlockSpec((1, 6, d), lambda i: (i * tm // tokens_per_mod, 0, 0)),
                  _resident(nw.shape), _resident(w_proj_t.shape), _resident(w_proj.shape),
                  _resident(w_gate.shape), _resident(b_gate.shape)],
        out_specs=[tok_t(nqk), tok(nqk), tok_t(nv), tok(nv), tok(GATE_LANES),
                   pl.BlockSpec((tm // ML_CHUNK,) + gr_shape[1:], lambda i: (i, 0, 0, 0))],
        out_shape=[jax.ShapeDtypeStruct((nqk, n), bf16), jax.ShapeDtypeStruct((n, nqk), bf16),
                   jax.ShapeDtypeStruct((nv, n), bf16), jax.ShapeDtypeStruct((n, nv), bf16),
                   jax.ShapeDtypeStruct((n, GATE_LANES), f32), jax.ShapeDtypeStruct(gr_shape, f32)],
        compiler_params=_params("parallel"),
        name="mlstm_proj",
    )(x2, mod, nw, w_proj_t, w_proj, w_gate, b_gate)


def _ml_chunk(qt_ref, k_ref, vt_ref, ac_ref, gr_ref, o_ref, s_ref, m_ref, ch, reverse):
    L = ML_CHUNK
    rows = slice(ch * L, (ch + 1) * L)
    emit = o_ref is not None
    if emit:
        s_idx = lax.broadcasted_iota(jnp.int32, (L, L), 0)
        t_idx = lax.broadcasted_iota(jnp.int32, (L, L), 1)
        mask = (s_idx >= t_idx) if reverse else (s_idx <= t_idx)
        zero_half = jnp.zeros((ML_DQK, L), bf16)
    first_row = lax.broadcasted_iota(jnp.int32, (STATE_ROWS - ML_DV, L), 0) == 0
    for pair in range(ML_HEADS // 2):
        k_pair = k_ref[rows, pair * 2 * ML_DQK:(pair + 1) * 2 * ML_DQK]
        if emit:
            qt0 = qt_ref[(2 * pair) * ML_DQK:(2 * pair + 1) * ML_DQK, rows]
            qt1 = qt_ref[(2 * pair + 1) * ML_DQK:(2 * pair + 2) * ML_DQK, rows]
            q_rhs = (jnp.concatenate([qt0, zero_half], axis=0), jnp.concatenate([zero_half, qt1], axis=0))
            kq = _dot(k_pair, jnp.concatenate(q_rhs, axis=1))
        for e in range(2):
            hd = 2 * pair + e
            g = (ML_HEADS if reverse else 0) + hd
            vt_h = vt_ref[hd * ML_DV:(hd + 1) * ML_DV, rows]
            a_row = gr_ref[ch, 0, g:g + 1, :]
            cm_row = gr_ref[ch, 1, g:g + 1, :]
            am_row = gr_ref[ch, 2, g:g + 1, :]
            m_row = m_ref[hd]
            state = s_ref[hd]
            if emit:
                b_row = gr_ref[ch, 3, g:g + 1, :]
                a_col = ac_ref[rows, g:g + 1]
                p_mat = jnp.exp(jnp.where(mask, a_col - cm_row, NEG_BIG))
                s_mat = kq[:, e * L:(e + 1) * L] * p_mat
                intra = _dot(vt_h, s_mat.astype(bf16))
                inter = _dot(state.astype(bf16), q_rhs[e])
                m_run = jnp.maximum(m_row, cm_row)
                r_intra = jnp.exp(cm_row - m_run)
                r_inter = jnp.exp(m_row - m_run)
                num = intra * r_intra + inter[:ML_DV] * r_inter
                den = jnp.sum(s_mat, axis=0, keepdims=True) * r_intra + inter[ML_DV:ML_DV + 1] * r_inter
                inv = 1.0 / jnp.maximum(jnp.abs(den), jnp.exp(-(b_row + m_run)))
                o_ref[hd * ML_DV:(hd + 1) * ML_DV, rows] = (num * inv).astype(o_ref.dtype)
            m_last = jnp.maximum(m_row, am_row)
            wk = jnp.exp(a_row - am_row)
            lhs = jnp.concatenate([(vt_h.astype(f32) * wk).astype(bf16),
                                   jnp.where(first_row, wk, 0.0).astype(bf16)], axis=0)
            kv = _dot(lhs, k_pair)
            s_ref[hd] = jnp.exp(m_row - m_last) * state + jnp.exp(am_row - m_last) * kv
            m_ref[hd] = gr_ref[ch, 4, g:g + 1, :] + m_last


def _ml_scan_kernel(kc_ref, vtc_ref, grc_ref, qtx_ref, kx_ref, vtx_ref, acx_ref, grx_ref,
                    o_ref, s_ref, m_ref, *, reverse):
    j = pl.program_id(1)

    def run(qt_ref, k_ref, vt_ref, ac_ref, gr_ref, out_ref):
        nch = k_ref.shape[0] // ML_CHUNK
        order = range(nch - 1, -1, -1) if reverse else range(nch)
        for ch in order:
            _ml_chunk(qt_ref, k_ref, vt_ref, ac_ref, gr_ref, out_ref, s_ref, m_ref, ch, reverse)

    @pl.when(j == 0)
    def _():
        s_ref[...] = jnp.zeros_like(s_ref)
        m_ref[...] = jnp.zeros_like(m_ref)
        run(None, kc_ref, vtc_ref, None, grc_ref, None)

    @pl.when(j > 0)
    def _():
        run(qtx_ref, kx_ref, vtx_ref, acx_ref, grx_ref, o_ref)


def _ml_scan(ctx_parts, lat_parts, *, batch, reverse):
    kc, vtc, grc = ctx_parts
    qtx, kx, vtx, acx, grx = lat_parts
    t_ctx = kc.shape[0] // batch
    t_lat = kx.shape[0] // batch
    tb = SCAN_BLOCK
    assert t_lat % tb == 0 and tb % ML_CHUNK == 0 and t_ctx % ML_CHUNK == 0
    nblk = t_lat // tb
    nqk = ML_HEADS * ML_DQK
    nv = ML_HEADS * ML_DV
    gr_tail = (N_GATE_ROWS, 2 * ML_HEADS, ML_CHUNK)

    def lat_blk(b, j):
        step = jnp.maximum(j - 1, 0)
        return b * nblk + (nblk - 1 - step if reverse else step)

    lat_tok = lambda width: pl.BlockSpec((tb, width), lambda b, j: (lat_blk(b, j), 0))
    lat_tok_t = lambda width: pl.BlockSpec((width, tb), lambda b, j: (0, lat_blk(b, j)))
    return pl.pallas_call(
        functools.partial(_ml_scan_kernel, reverse=reverse),
        grid=(batch, 1 + nblk),
        in_specs=[pl.BlockSpec((t_ctx, nqk), lambda b, j: (b, 0)),
                  pl.BlockSpec((nv, t_ctx), lambda b, j: (0, b)),
                  pl.BlockSpec((t_ctx // ML_CHUNK,) + gr_tail, lambda b, j: (b, 0, 0, 0)),
                  lat_tok_t(nqk), lat_tok(nqk), lat_tok_t(nv), lat_tok(GATE_LANES),
                  pl.BlockSpec((tb // ML_CHUNK,) + gr_tail, lambda b, j: (lat_blk(b, j), 0, 0, 0))],
        out_specs=lat_tok_t(nv),
        out_shape=jax.ShapeDtypeStruct((nv, batch * t_lat), bf16),
        scratch_shapes=[pltpu.VMEM((ML_HEADS, STATE_ROWS, 2 * ML_DQK), f32),
                        pltpu.VMEM((ML_HEADS, 1, ML_CHUNK), f32)],
        compiler_params=_params("parallel", "arbitrary"),
        name="mlstm_scan_bwd" if reverse else "mlstm_scan_fwd",
    )(kc, vtc, grc, qtx, kx, vtx, acx, grx)


def _ml_readout_kernel(hf_ref, hb_ref, og_ref, x_ref, mod_ref, nw_ref, mnw_ref, wout_ref, o_ref, z_ref):
    x = x_ref[...]
    mod = mod_ref[0]
    nw = nw_ref[...]
    mnw = mnw_ref[...]
    for hd in range(ML_HEADS):
        vs = slice(hd * ML_DV, (hd + 1) * ML_DV)
        hs = hf_ref[vs, :].astype(f32) + hb_ref[vs, :].astype(f32)
        hn = hs * lax.rsqrt(jnp.mean(hs * hs, axis=0, keepdims=True) + EPS)
        z_ref[:, vs] = (jax.nn.sigmoid(og_ref[:, vs].astype(f32)) * (hn.T * mnw[:, vs])).astype(bf16)
    yx = _dot(z_ref[...], wout_ref[...])
    o_ref[...] = x + mod[2:3] * _rms(yx, nw[1:2])


def _ml_readout(hf_t, hb_t, og, x2, mod, nw, ml_norm_w, w_out, *, tokens_per_mod):
    n, d = x2.shape
    tm = TOKEN_TILE
    nv = ML_HEADS * ML_DV
    assert n % tm == 0 and tokens_per_mod % tm == 0
    tok = lambda width: pl.BlockSpec((tm, width), lambda i: (i, 0))
    tok_t = lambda width: pl.BlockSpec((width, tm), lambda i: (0, i))
    return pl.pallas_call(
        _ml_readout_kernel,
        grid=(n // tm,),
        in_specs=[tok_t(nv), tok_t(nv), tok(nv), tok(d),
                  pl.BlockSpec((1, 6, d), lambda i: (i * tm // tokens_per_mod, 0, 0)),
                  _resident(nw.shape), _resident(ml_norm_w.shape), _resident(w_out.shape)],
        out_specs=tok(d),
        out_shape=jax.ShapeDtypeStruct((n, d), f32),
        scratch_shapes=[pltpu.VMEM((tm, nv), bf16)],
        compiler_params=_params("parallel"),
        name="mlstm_readout",
    )(hf_t, hb_t, og, x2, mod, nw, ml_norm_w, w_out)


def kernel(x, c, ctx, c_ctx, norm_w, mod_w, mod_b, mlp_w1, mlp_w2, conv_w_in, conv_w, conv_w_out,
           ml_w_qkvo, ml_w_if, ml_b_if, ml_norm_w, ml_w_out):
    batch, seq, d = x.shape
    t_ctx = ctx.shape[1]
    depth = norm_w.shape[0]
    assert depth == 2 and seq % GRID_W == 0 and d == ML_HEADS * ML_DV

    pad_rows = -(batch + 1) % 8
    cc = jnp.concatenate([c, c_ctx[None, :], jnp.zeros((pad_rows, d), f32)], axis=0)
    mod = _adaln(cc, mod_w, mod_b).reshape(depth, batch + 1 + pad_rows, 6, d)

    xs = x.reshape(batch * seq, d)
    cs = ctx.reshape(batch * t_ctx, d)

    mx, mc = mod[0, :batch], mod[0, batch:batch + 1]
    w_in = conv_w_in[0].astype(bf16)
    w_out = conv_w_out[0].astype(bf16)
    w1 = mlp_w1[0].astype(bf16)
    w2 = mlp_w2[0].astype(bf16)
    xs = _conv_mixer(xs, mx, norm_w[0], w_in, conv_w[0], w_out, row_len=GRID_W, tokens_per_mod=seq)
    xs = _mlp(xs, mx, norm_w[0], w1, w2, tokens_per_mod=seq)
    cs = _conv_mixer(cs, mc, norm_w[0], w_in, conv_w[0], w_out, row_len=t_ctx, tokens_per_mod=batch * t_ctx)
    cs = _mlp(cs, mc, norm_w[0], w1, w2, tokens_per_mod=batch * t_ctx)

    mx, mc = mod[1, :batch], mod[1, batch:batch + 1]
    nh = ML_HEADS
    nqk = nh * ML_DQK
    wq, wk, wv, wo = (ml_w_qkvo[0][:, lo:hi] for lo, hi in
                      ((0, nqk), (nqk, 2 * nqk), (2 * nqk, 2 * nqk + d), (2 * nqk + d, 2 * nqk + 2 * d)))
    w_proj_t = jnp.concatenate([wq, wv], axis=1).T.astype(bf16)
    w_proj = jnp.concatenate([wk, wo], axis=1).astype(bf16)
    zpad = jnp.zeros((d, GATE_LANES - 2 * nh), f32)
    w_gate = jnp.concatenate([ml_w_if[0, 0, :, :nh], ml_w_if[0, 1, :, :nh], zpad,
                              ml_w_if[0, 0, :, nh:], ml_w_if[0, 1, :, nh:], zpad], axis=1).astype(bf16)
    bpad = jnp.zeros((GATE_LANES - 2 * nh,), f32)
    b_gate = jnp.concatenate([ml_b_if[0, 0, :nh], ml_b_if[0, 1, :nh], bpad,
                              ml_b_if[0, 0, nh:], ml_b_if[0, 1, nh:], bpad])[None, :]
    qtx, kx, vtx, ogx, acx, grx = _ml_proj(xs, mx, norm_w[1], w_proj_t, w_proj, w_gate, b_gate,
                                           tokens_per_mod=seq)
    _, kc, vtc, _, _, grc = _ml_proj(cs, mc, norm_w[1], w_proj_t, w_proj, w_gate, b_gate,
                                     tokens_per_mod=batch * t_ctx)
    hf_t = _ml_scan((kc, vtc, grc), (qtx, kx, vtx, acx, grx), batch=batch, reverse=False)
    hb_t = _ml_scan((kc, vtc, grc), (qtx, kx, vtx, acx, grx), batch=batch, reverse=True)
    xs = _ml_readout(hf_t, hb_t, ogx, xs, mx, norm_w[1], ml_norm_w[0][None, :], ml_w_out[0].astype(bf16),
                     tokens_per_mod=seq)
    xs = _mlp(xs, mx, norm_w[1], mlp_w1[1].astype(bf16), mlp_w2[1].astype(bf16), tokens_per_mod=seq)
    return xs.reshape(batch, seq, d)
```

```python
import functools

import jax
import jax.numpy as jnp
from jax import lax
from jax.experimental import pallas as pl
from jax.experimental.pallas import tpu as pltpu

EPS = 1e-6
GRID_W = 64
ML_HEADS = 8
ML_DV = 128
ML_DQK = 64
ML_CHUNK = 128
GATE_LANES = 128
N_GATE_ROWS = 5
STATE_ROWS = ML_DV + 16
NEG_BIG = -1e30

VMEM_LIMIT_BYTES = 56 * 1024 * 1024
TOKEN_TILE = 512
SCAN_BLOCK = 1024
FF_CHUNK = 1024
CONV_COL_CHUNK = 512
ADALN_COL_TILE = 1536

f32 = jnp.float32
bf16 = jnp.bfloat16


def _dot(a, b):
    return jnp.dot(a, b, preferred_element_type=f32)


def _rms(x, w):
    return x * lax.rsqrt(jnp.mean(x * x, axis=-1, keepdims=True) + EPS) * w


def _resident(shape):
    nd = len(shape)
    return pl.BlockSpec(shape, lambda *_: (0,) * nd, pipeline_mode=pl.Buffered(1))


def _params(*sem):
    return pltpu.CompilerParams(dimension_semantics=sem, vmem_limit_bytes=VMEM_LIMIT_BYTES)


def _adaln_kernel(c_ref, w_ref, b_ref, o_ref):
    c = c_ref[...]
    s = c * jax.nn.sigmoid(c)
    w = w_ref[0]
    sh = s.astype(bf16)
    sl = (s - sh.astype(f32)).astype(bf16)
    wh = w.astype(bf16)
    wl = (w - wh.astype(f32)).astype(bf16)
    o_ref[0] = _dot(sh, wh) + _dot(sh, wl) + _dot(sl, wh) + b_ref[0]


def _adaln(cc, mod_w, mod_b):
    depth, d, n = mod_w.shape
    rows = cc.shape[0]
    tn = ADALN_COL_TILE
    return pl.pallas_call(
        _adaln_kernel,
        grid=(depth, n // tn),
        in_specs=[pl.BlockSpec((rows, d), lambda i, j: (0, 0)),
                  pl.BlockSpec((1, d, tn), lambda i, j: (i, 0, j)),
                  pl.BlockSpec((1, 1, tn), lambda i, j: (i, 0, j))],
        out_specs=pl.BlockSpec((1, rows, tn), lambda i, j: (i, 0, j)),
        out_shape=jax.ShapeDtypeStruct((depth, rows, n), f32),
        compiler_params=_params("arbitrary", "arbitrary"),
        name="adaln",
    )(cc, mod_w, mod_b.reshape(depth, 1, n))


def _conv_mixer_kernel(x_ref, mod_ref, nw_ref, win_ref, cw_ref, wout_ref, w1_ref, w2_ref,
                       o_ref, z_ref, acc_ref, *, row_len):
    x = x_ref[...]
    mod = mod_ref[0]
    nw = nw_ref[...]
    tm, d = x.shape
    h = (_rms(x, nw[0:1]) * (1.0 + mod[1:2]) + mod[0:1]).astype(bf16)
    cw = cw_ref[...]
    cc = CONV_COL_CHUNK
    pos = lax.broadcasted_iota(jnp.int32, (tm, cc), 0) % row_len
    for j in range(d // cc):
        cs = slice(j * cc, (j + 1) * cc)
        b_gate = _dot(h, win_ref[:, j * cc:(j + 1) * cc])
        c_gate = _dot(h, win_ref[:, d + j * cc:d + (j + 1) * cc])
        u = c_gate * _dot(h, win_ref[:, 2 * d + j * cc:2 * d + (j + 1) * cc])
        u_prev = jnp.where(pos == 0, 0.0, pltpu.roll(u, 1, 0))
        u_next = jnp.where(pos == row_len - 1, 0.0, pltpu.roll(u, tm - 1, 0))
        y = cw[0:1, cs] * u_prev + cw[1:2, cs] * u + cw[2:3, cs] * u_next
        z_ref[:, cs] = (b_gate * y).astype(bf16)
    yx = _dot(z_ref[...], wout_ref[...])
    _mlp_residual(x + mod[2:3] * _rms(yx, nw[1:2]), mod, nw, w1_ref, w2_ref, o_ref, acc_ref)


def _conv_mixer_mlp(x2, mod, nw, w_in, conv_w, w_out, w1, w2, *, row_len, tokens_per_mod):
    n, d = x2.shape
    tm = TOKEN_TILE
    assert n % tm == 0 and tm % row_len == 0 and tokens_per_mod % tm == 0
    return pl.pallas_call(
        functools.partial(_conv_mixer_kernel, row_len=row_len),
        grid=(n // tm,),
        in_specs=[pl.BlockSpec((tm, d), lambda i: (i, 0)),
                  pl.BlockSpec((1, 6, d), lambda i: (i * tm // tokens_per_mod, 0, 0)),
                  _resident(nw.shape), _resident(w_in.shape), _resident(conv_w.shape),
                  _resident(w_out.shape), _resident(w1.shape), _resident(w2.shape)],
        out_specs=pl.BlockSpec((tm, d), lambda i: (i, 0)),
        out_shape=jax.ShapeDtypeStruct((n, d), f32),
        scratch_shapes=[pltpu.VMEM((tm, d), bf16), pltpu.VMEM((tm, d), f32)],
        compiler_params=_params("parallel"),
        name="conv_mixer_mlp",
    )(x2, mod, nw, w_in, conv_w, w_out, w1, w2)


def _mlp_residual(x, mod, nw, w1_ref, w2_ref, o_ref, acc_ref):
    h = (_rms(x, nw[2:3]) * (1.0 + mod[4:5]) + mod[3:4]).astype(bf16)
    ff = w1_ref.shape[1]
    for j in range(ff // FF_CHUNK):
        a = jnp.maximum(_dot(h, w1_ref[:, j * FF_CHUNK:(j + 1) * FF_CHUNK]), 0.0)
        part = _dot((a * a).astype(bf16), w2_ref[j * FF_CHUNK:(j + 1) * FF_CHUNK, :])
        if j == 0:
            acc_ref[...] = part
        else:
            acc_ref[...] += part
    o_ref[...] = x + mod[5:6] * _rms(acc_ref[...], nw[3:4])


def _log_sigmoid(x):
    return jnp.minimum(x, 0.0) - jnp.log1p(jnp.exp(-jnp.abs(x)))


def _ml_proj_kernel(x_ref, mod_ref, nw_ref, wpt_ref, wp_ref, wg_ref, bg_ref,
                    qt_ref, k_ref, vt_ref, o_ref, ac_ref, gr_ref):
    x = x_ref[...]
    mod = mod_ref[0]
    nw = nw_ref[...]
    tm = x.shape[0]
    h32 = _rms(x, nw[0:1]) * (1.0 + mod[1:2]) + mod[0:1]
    h = h32.astype(bf16)
    h_t = h32.T.astype(bf16)
    nqk = ML_HEADS * ML_DQK
    qt_ref[...] = _dot(wpt_ref[0:nqk, :], h_t) * (ML_DQK ** -0.5)
    vt_ref[...] = _dot(wpt_ref[nqk:, :], h_t).astype(bf16)
    k_ref[...] = _dot(h, wp_ref[:, 0:nqk]).astype(bf16)
    o_ref[...] = _dot(h, wp_ref[:, nqk:]).astype(bf16)

    g = _dot(h, wg_ref[...]) + bg_ref[...]
    ig = g[:, :GATE_LANES]
    lf = _log_sigmoid(g[:, GATE_LANES:])
    pos = lax.broadcasted_iota(jnp.int32, (tm, GATE_LANES), 0) % ML_CHUNK
    is_fwd = lax.broadcasted_iota(jnp.int32, (tm, GATE_LANES), 1) < ML_HEADS

    def chunk_scan(v, op, fill):
        pre, suf = v, v
        step = 1
        while step < ML_CHUNK:
            pre = op(pre, jnp.where(pos >= step, pltpu.roll(pre, step, 0), fill))
            suf = op(suf, jnp.where(pos < ML_CHUNK - step, pltpu.roll(suf, tm - step, 0), fill))
            step *= 2
        return pre, suf

    b_pre, b_suf = chunk_scan(lf, jnp.add, 0.0)
    b = jnp.where(is_fwd, b_pre, b_suf)
    tot = b_pre + b_suf - lf
    a = ig - b
    a_pre, a_suf = chunk_scan(a, jnp.maximum, NEG_BIG)
    cm = jnp.where(is_fwd, a_pre, a_suf)
    am = jnp.maximum(a_pre, a_suf)
    ac_ref[...] = a
    for i, arr in enumerate((a, cm, am, b, tot)):
        arr_t = arr.T
        for ch in range(tm // ML_CHUNK):
            gr_ref[ch, i] = arr_t[0:2 * ML_HEADS, ch * ML_CHUNK:(ch + 1) * ML_CHUNK]


def _ml_proj(x2, mod, nw, w_proj_t, w_proj, w_gate, b_gate, *, tokens_per_mod):
    n, d = x2.shape
    tm = TOKEN_TILE
    assert n % tm == 0 and tokens_per_mod % tm == 0 and tm % ML_CHUNK == 0
    nqk = ML_HEADS * ML_DQK
    nv = ML_HEADS * ML_DV
    tok = lambda width: pl.BlockSpec((tm, width), lambda i: (i, 0))
    tok_t = lambda width: pl.BlockSpec((width, tm), lambda i: (0, i))
    gr_shape = (n // ML_CHUNK, N_GATE_ROWS, 2 * ML_HEADS, ML_CHUNK)
    return pl.pallas_call(
        _ml_proj_kernel,
        grid=(n // tm,),
        in_specs=[tok(d),
                  pl.BlockSpec((1, 6, d), lambda i: (i * tm // tokens_per_mod, 0, 0)),
                  _resident(nw.shape), _resident(w_proj_t.shape), _resident(w_proj.shape),
                  _resident(w_gate.shape), _resident(b_gate.shape)],
        out_specs=[tok_t(nqk), tok(nqk), tok_t(nv), tok(nv), tok(GATE_LANES),
                   pl.BlockSpec((tm // ML_CHUNK,) + gr_shape[1:], lambda i: (i, 0, 0, 0))],
        out_shape=[jax.ShapeDtypeStruct((nqk, n), f32), jax.ShapeDtypeStruct((n, nqk), bf16),
                   jax.ShapeDtypeStruct((nv, n), bf16), jax.ShapeDtypeStruct((n, nv), bf16),
                   jax.ShapeDtypeStruct((n, GATE_LANES), f32), jax.ShapeDtypeStruct(gr_shape, f32)],
        compiler_params=_params("parallel"),
        name="mlstm_proj",
    )(x2, mod, nw, w_proj_t, w_proj, w_gate, b_gate)


def _ml_chunk(qt_ref, k_ref, vt_ref, ac_ref, gr_ref, o_ref, s_ref, m_ref, rhs_ref, ch, reverse):
    L = ML_CHUNK
    rows = slice(ch * L, (ch + 1) * L)
    emit = o_ref is not None
    if emit:
        s_idx = lax.broadcasted_iota(jnp.int32, (L, L), 0)
        t_idx = lax.broadcasted_iota(jnp.int32, (L, L), 1)
        mask = (s_idx >= t_idx) if reverse else (s_idx <= t_idx)
        zero_half = jnp.zeros((ML_DQK, L), bf16)

        def pad_half(q_half, e):
            return jnp.concatenate([q_half, zero_half] if e == 0 else [zero_half, q_half], axis=0)

    first_row = lax.broadcasted_iota(jnp.int32, (STATE_ROWS - ML_DV, L), 0) == 0
    ones_row = jnp.where(first_row, 1.0, 0.0).astype(bf16)
    g0 = ML_HEADS if reverse else 0

    def gate(i, hd):
        return gr_ref[ch, i, g0 + hd:g0 + hd + 1, :]

    def k_pair(pair):
        return k_ref[rows, pair * 2 * ML_DQK:(pair + 1) * 2 * ML_DQK]

    if emit:
        for pair in range(ML_HEADS // 2):
            qt = (qt_ref[(2 * pair) * ML_DQK:(2 * pair + 1) * ML_DQK, rows],
                  qt_ref[(2 * pair + 1) * ML_DQK:(2 * pair + 2) * ML_DQK, rows])
            kq = _dot(k_pair(pair), jnp.concatenate([pad_half(qt[0].astype(bf16), 0),
                                                     pad_half(qt[1].astype(bf16), 1)], axis=1))
            for e in range(2):
                hd = 2 * pair + e
                m_row = m_ref[hd]
                m_run = jnp.maximum(m_row, gate(1, hd))
                a_col = ac_ref[rows, g0 + hd:g0 + hd + 1]
                s_mat = kq[:, e * L:(e + 1) * L] * jnp.exp(jnp.where(mask, a_col - m_run, NEG_BIG))
                rhs_ref[hd, 0:L, :] = s_mat.astype(bf16)
                rhs_ref[hd, L:2 * L, :] = pad_half((qt[e] * jnp.exp(m_row - m_run)).astype(bf16), e)

    for hd in range(ML_HEADS):
        vt_h = vt_ref[hd * ML_DV:(hd + 1) * ML_DV, rows]
        m_row = m_ref[hd]
        state = s_ref[hd]
        if emit:
            lhs = jnp.concatenate([jnp.concatenate([vt_h, ones_row], axis=0), state.astype(bf16)], axis=1)
            res = _dot(lhs, rhs_ref[hd])
            floor = jnp.exp(-(gate(3, hd) + jnp.maximum(m_row, gate(1, hd))))
            inv = 1.0 / jnp.maximum(jnp.abs(res[ML_DV:ML_DV + 1]), floor)
            o_ref[hd * ML_DV:(hd + 1) * ML_DV, rows] = (res[:ML_DV] * inv).astype(o_ref.dtype)
        m_last = jnp.maximum(m_row, gate(2, hd))
        wk = jnp.exp(gate(0, hd) - m_last)
        lhs = jnp.concatenate([vt_h * wk.astype(bf16), jnp.where(first_row, wk, 0.0).astype(bf16)], axis=0)
        kv = _dot(lhs, k_pair(hd // 2))
        s_ref[hd] = jnp.exp(m_row - m_last) * state + kv
        m_ref[hd] = gate(4, hd) + m_last


def _ml_scan_kernel(kc_ref, vtc_ref, grc_ref, qtx_ref, kx_ref, vtx_ref, acx_ref, grx_ref,
                    o_ref, s_ref, m_ref, rhs_ref, *, reverse):
    j = pl.program_id(1)

    def run(qt_ref, k_ref, vt_ref, ac_ref, gr_ref, out_ref):
        nch = k_ref.shape[0] // ML_CHUNK
        order = range(nch - 1, -1, -1) if reverse else range(nch)
        for ch in order:
            _ml_chunk(qt_ref, k_ref, vt_ref, ac_ref, gr_ref, out_ref, s_ref, m_ref, rhs_ref, ch, reverse)

    @pl.when(j == 0)
    def _():
        s_ref[...] = jnp.zeros_like(s_ref)
        m_ref[...] = jnp.zeros_like(m_ref)
        run(None, kc_ref, vtc_ref, None, grc_ref, None)

    @pl.when(j > 0)
    def _():
        run(qtx_ref, kx_ref, vtx_ref, acx_ref, grx_ref, o_ref)


def _ml_scan(ctx_parts, lat_parts, *, batch, reverse):
    kc, vtc, grc = ctx_parts
    qtx, kx, vtx, acx, grx = lat_parts
    t_ctx = kc.shape[0] // batch
    t_lat = kx.shape[0] // batch
    tb = SCAN_BLOCK
    assert t_lat % tb == 0 and tb % ML_CHUNK == 0 and t_ctx % ML_CHUNK == 0
    nblk = t_lat // tb
    nqk = ML_HEADS * ML_DQK
    nv = ML_HEADS * ML_DV
    gr_tail = (N_GATE_ROWS, 2 * ML_HEADS, ML_CHUNK)

    def lat_blk(b, j):
        step = jnp.maximum(j - 1, 0)
        return b * nblk + (nblk - 1 - step if reverse else step)

    lat_tok = lambda width: pl.BlockSpec((tb, width), lambda b, j: (lat_blk(b, j), 0))
    lat_tok_t = lambda width: pl.BlockSpec((width, tb), lambda b, j: (0, lat_blk(b, j)))
    return pl.pallas_call(
        functools.partial(_ml_scan_kernel, reverse=reverse),
        grid=(batch, 1 + nblk),
        in_specs=[pl.BlockSpec((t_ctx, nqk), lambda b, j: (b, 0)),
                  pl.BlockSpec((nv, t_ctx), lambda b, j: (0, b)),
                  pl.BlockSpec((t_ctx // ML_CHUNK,) + gr_tail, lambda b, j: (b, 0, 0, 0)),
                  lat_tok_t(nqk), lat_tok(nqk), lat_tok_t(nv), lat_tok(GATE_LANES),
                  pl.BlockSpec((tb // ML_CHUNK,) + gr_tail, lambda b, j: (lat_blk(b, j), 0, 0, 0))],
        out_specs=lat_tok_t(nv),
        out_shape=jax.ShapeDtypeStruct((nv, batch * t_lat), bf16),
        scratch_shapes=[pltpu.VMEM((ML_HEADS, STATE_ROWS, 2 * ML_DQK), f32),
                        pltpu.VMEM((ML_HEADS, 1, ML_CHUNK), f32),
                        pltpu.VMEM((ML_HEADS, 2 * ML_CHUNK, ML_CHUNK), bf16)],
        compiler_params=_params("parallel", "arbitrary"),
        name="mlstm_scan_bwd" if reverse else "mlstm_scan_fwd",
    )(kc, vtc, grc, qtx, kx, vtx, acx, grx)


def _ml_readout_kernel(hf_ref, hb_ref, og_ref, x_ref, mod_ref, nw_ref, mnw_ref, wout_ref, w1_ref, w2_ref,
                       o_ref, z_ref, acc_ref):
    x = x_ref[...]
    mod = mod_ref[0]
    nw = nw_ref[...]
    mnw = mnw_ref[...]
    for hd in range(ML_HEADS):
        vs = slice(hd * ML_DV, (hd + 1) * ML_DV)
        hs = hf_ref[vs, :].astype(f32) + hb_ref[vs, :].astype(f32)
        hn = hs * lax.rsqrt(jnp.mean(hs * hs, axis=0, keepdims=True) + EPS)
        z_ref[:, vs] = (jax.nn.sigmoid(og_ref[:, vs].astype(f32)) * (hn.T * mnw[:, vs])).astype(bf16)
    yx = _dot(z_ref[...], wout_ref[...])
    _mlp_residual(x + mod[2:3] * _rms(yx, nw[1:2]), mod, nw, w1_ref, w2_ref, o_ref, acc_ref)


def _ml_readout_mlp(hf_t, hb_t, og, x2, mod, nw, ml_norm_w, w_out, w1, w2, *, tokens_per_mod):
    n, d = x2.shape
    tm = TOKEN_TILE
    nv = ML_HEADS * ML_DV
    assert n % tm == 0 and tokens_per_mod % tm == 0
    tok = lambda width: pl.BlockSpec((tm, width), lambda i: (i, 0))
    tok_t = lambda width: pl.BlockSpec((width, tm), lambda i: (0, i))
    return pl.pallas_call(
        _ml_readout_kernel,
        grid=(n // tm,),
        in_specs=[tok_t(nv), tok_t(nv), tok(nv), tok(d),
                  pl.BlockSpec((1, 6, d), lambda i: (i * tm // tokens_per_mod, 0, 0)),
                  _resident(nw.shape), _resident(ml_norm_w.shape), _resident(w_out.shape),
                  _resident(w1.shape), _resident(w2.shape)],
        out_specs=tok(d),
        out_shape=jax.ShapeDtypeStruct((n, d), f32),
        scratch_shapes=[pltpu.VMEM((tm, nv), bf16), pltpu.VMEM((tm, d), f32)],
        compiler_params=_params("parallel"),
        name="mlstm_readout_mlp",
    )(hf_t, hb_t, og, x2, mod, nw, ml_norm_w, w_out, w1, w2)


def kernel(x, c, ctx, c_ctx, norm_w, mod_w, mod_b, mlp_w1, mlp_w2, conv_w_in, conv_w, conv_w_out,
           ml_w_qkvo, ml_w_if, ml_b_if, ml_norm_w, ml_w_out):
    batch, seq, d = x.shape
    t_ctx = ctx.shape[1]
    depth = norm_w.shape[0]
    assert depth == 2 and seq % GRID_W == 0 and d == ML_HEADS * ML_DV

    pad_rows = -(batch + 1) % 8
    cc = jnp.concatenate([c, c_ctx[None, :], jnp.zeros((pad_rows, d), f32)], axis=0)
    mod = _adaln(cc, mod_w, mod_b).reshape(depth, batch + 1 + pad_rows, 6, d)

    xs = x.reshape(batch * seq, d)
    cs = ctx.reshape(batch * t_ctx, d)

    mx, mc = mod[0, :batch], mod[0, batch:batch + 1]
    w_in = conv_w_in[0].astype(bf16)
    w_out = conv_w_out[0].astype(bf16)
    w1 = mlp_w1[0].astype(bf16)
    w2 = mlp_w2[0].astype(bf16)
    xs = _conv_mixer_mlp(xs, mx, norm_w[0], w_in, conv_w[0], w_out, w1, w2, row_len=GRID_W, tokens_per_mod=seq)
    cs = _conv_mixer_mlp(cs, mc, norm_w[0], w_in, conv_w[0], w_out, w1, w2, row_len=t_ctx,
                         tokens_per_mod=batch * t_ctx)

    mx, mc = mod[1, :batch], mod[1, batch:batch + 1]
    nh = ML_HEADS
    nqk = nh * ML_DQK
    wq, wk, wv, wo = (ml_w_qkvo[0][:, lo:hi] for lo, hi in
                      ((0, nqk), (nqk, 2 * nqk), (2 * nqk, 2 * nqk + d), (2 * nqk + d, 2 * nqk + 2 * d)))
    w_proj_t = jnp.concatenate([wq, wv], axis=1).T.astype(bf16)
    w_proj = jnp.concatenate([wk, wo], axis=1).astype(bf16)
    zpad = jnp.zeros((d, GATE_LANES - 2 * nh), f32)
    w_gate = jnp.concatenate([ml_w_if[0, 0, :, :nh], ml_w_if[0, 1, :, :nh], zpad,
                              ml_w_if[0, 0, :, nh:], ml_w_if[0, 1, :, nh:], zpad], axis=1).astype(bf16)
    bpad = jnp.zeros((GATE_LANES - 2 * nh,), f32)
    b_gate = jnp.concatenate([ml_b_if[0, 0, :nh], ml_b_if[0, 1, :nh], bpad,
                              ml_b_if[0, 0, nh:], ml_b_if[0, 1, nh:], bpad])[None, :]
    qtx, kx, vtx, ogx, acx, grx = _ml_proj(xs, mx, norm_w[1], w_proj_t, w_proj, w_gate, b_gate,
                                           tokens_per_mod=seq)
    _, kc, vtc, _, _, grc = _ml_proj(cs, mc, norm_w[1], w_proj_t, w_proj, w_gate, b_gate,
                                     tokens_per_mod=batch * t_ctx)
    hf_t = _ml_scan((kc, vtc, grc), (qtx, kx, vtx, acx, grx), batch=batch, reverse=False)
    hb_t = _ml_scan((kc, vtc, grc), (qtx, kx, vtx, acx, grx), batch=batch, reverse=True)
    xs = _ml_readout_mlp(hf_t, hb_t, ogx, xs, mx, norm_w[1], ml_norm_w[0][None, :], ml_w_out[0].astype(bf16),
                         mlp_w1[1].astype(bf16), mlp_w2[1].astype(bf16), tokens_per_mod=seq)
    return xs.reshape(batch, seq, d)
```

```python
import functools

import jax
import jax.numpy as jnp
from jax import lax
from jax.experimental import pallas as pl
from jax.experimental.pallas import tpu as pltpu

EPS = 1e-6
GRID_W = 64
ML_HEADS = 8
ML_DV = 128
ML_DQK = 64
ML_CHUNK = 128
GATE_LANES = 128
N_GATE_ROWS = 5
STATE_ROWS = ML_DV + 16
NEG_BIG = -1e30

VMEM_LIMIT_BYTES = 56 * 1024 * 1024
TOKEN_TILE = 512
ROW_PARTS = 2
SCAN_BLOCK = 1024
FF_CHUNK = 1024
CONV_COL_CHUNK = 512
ADALN_COL_TILE = 1536

f32 = jnp.float32
bf16 = jnp.bfloat16


def _dot(a, b):
    return jnp.dot(a, b, preferred_element_type=f32)


def _rms(x, w):
    return x * lax.rsqrt(jnp.mean(x * x, axis=-1, keepdims=True) + EPS) * w


def _resident(shape):
    nd = len(shape)
    return pl.BlockSpec(shape, lambda *_: (0,) * nd, pipeline_mode=pl.Buffered(1))


def _params(*sem):
    return pltpu.CompilerParams(dimension_semantics=sem, vmem_limit_bytes=VMEM_LIMIT_BYTES)


def _adaln_kernel(c_ref, w_ref, b_ref, o_ref):
    c = c_ref[...]
    s = c * jax.nn.sigmoid(c)
    w = w_ref[0]
    sh = s.astype(bf16)
    sl = (s - sh.astype(f32)).astype(bf16)
    wh = w.astype(bf16)
    wl = (w - wh.astype(f32)).astype(bf16)
    o_ref[0] = _dot(sh, wh) + _dot(sh, wl) + _dot(sl, wh) + b_ref[0]


def _adaln(cc, mod_w, mod_b):
    depth, d, n = mod_w.shape
    rows = cc.shape[0]
    tn = ADALN_COL_TILE
    return pl.pallas_call(
        _adaln_kernel,
        grid=(depth, n // tn),
        in_specs=[pl.BlockSpec((rows, d), lambda i, j: (0, 0)),
                  pl.BlockSpec((1, d, tn), lambda i, j: (i, 0, j)),
                  pl.BlockSpec((1, 1, tn), lambda i, j: (i, 0, j))],
        out_specs=pl.BlockSpec((1, rows, tn), lambda i, j: (i, 0, j)),
        out_shape=jax.ShapeDtypeStruct((depth, rows, n), f32),
        compiler_params=_params("arbitrary", "arbitrary"),
        name="adaln",
    )(cc, mod_w, mod_b.reshape(depth, 1, n))


def _conv_mixer_kernel(x_ref, mod_ref, nw_ref, win_ref, cw_ref, wout_ref, w1_ref, w2_ref,
                       o_ref, z_ref, h_ref, x1_ref, acc_ref, *, row_len):
    mod = mod_ref[0]
    nw = nw_ref[...]
    tm, d = x_ref.shape
    cw = cw_ref[...]
    cc = CONV_COL_CHUNK
    parts = _row_parts(tm, row_len)
    pm = parts[0].stop
    pos = lax.broadcasted_iota(jnp.int32, (pm, cc), 0) % row_len

    for r in parts:
        h_ref[r, :] = (_rms(x_ref[r, :], nw[0:1]) * (1.0 + mod[1:2]) + mod[0:1]).astype(bf16)
    for r in parts:
        h = h_ref[r, :]
        for j in range(d // cc):
            cs = slice(j * cc, (j + 1) * cc)
            b_gate = _dot(h, win_ref[:, j * cc:(j + 1) * cc])
            c_gate = _dot(h, win_ref[:, d + j * cc:d + (j + 1) * cc])
            u = c_gate * _dot(h, win_ref[:, 2 * d + j * cc:2 * d + (j + 1) * cc])
            u_prev = jnp.where(pos == 0, 0.0, pltpu.roll(u, 1, 0))
            u_next = jnp.where(pos == row_len - 1, 0.0, pltpu.roll(u, pm - 1, 0))
            y = cw[0:1, cs] * u_prev + cw[1:2, cs] * u + cw[2:3, cs] * u_next
            z_ref[r, cs] = (b_gate * y).astype(bf16)
    for r in parts:
        _mixer_residual(x_ref[r, :], _dot(z_ref[r, :], wout_ref[...]), mod, nw, x1_ref, h_ref, r)
    for r in parts:
        _mlp_matmuls(h_ref, acc_ref, w1_ref, w2_ref, r)
    for r in parts:
        _mlp_out(x1_ref, acc_ref, o_ref, mod, nw, r)


def _conv_mixer_mlp(x2, mod, nw, w_in, conv_w, w_out, w1, w2, *, row_len, tokens_per_mod):
    n, d = x2.shape
    tm = TOKEN_TILE
    assert n % tm == 0 and tm % row_len == 0 and tokens_per_mod % tm == 0
    return pl.pallas_call(
        functools.partial(_conv_mixer_kernel, row_len=row_len),
        grid=(n // tm,),
        in_specs=[pl.BlockSpec((tm, d), lambda i: (i, 0)),
                  pl.BlockSpec((1, 6, d), lambda i: (i * tm // tokens_per_mod, 0, 0)),
                  _resident(nw.shape), _resident(w_in.shape), _resident(conv_w.shape),
                  _resident(w_out.shape), _resident(w1.shape), _resident(w2.shape)],
        out_specs=pl.BlockSpec((tm, d), lambda i: (i, 0)),
        out_shape=jax.ShapeDtypeStruct((n, d), f32),
        scratch_shapes=[pltpu.VMEM((tm, d), bf16), pltpu.VMEM((tm, d), bf16),
                        pltpu.VMEM((tm, d), f32), pltpu.VMEM((tm, d), f32)],
        compiler_params=_params("parallel"),
        name="conv_mixer_mlp",
    )(x2, mod, nw, w_in, conv_w, w_out, w1, w2)


def _row_parts(tm, multiple):
    pm = tm // ROW_PARTS
    assert pm * ROW_PARTS == tm and pm % multiple == 0 and pm % 16 == 0
    return [slice(i * pm, (i + 1) * pm) for i in range(ROW_PARTS)]


def _mixer_residual(x, y_mix, mod, nw, x1_ref, h_ref, r):
    x1 = x + mod[2:3] * _rms(y_mix, nw[1:2])
    x1_ref[r, :] = x1
    h_ref[r, :] = (_rms(x1, nw[2:3]) * (1.0 + mod[4:5]) + mod[3:4]).astype(bf16)


def _mlp_matmuls(h_ref, acc_ref, w1_ref, w2_ref, r):
    ff = w1_ref.shape[1]
    h = h_ref[r, :]
    for j in range(ff // FF_CHUNK):
        a = jnp.maximum(_dot(h, w1_ref[:, j * FF_CHUNK:(j + 1) * FF_CHUNK]), 0.0)
        part = _dot((a * a).astype(bf16), w2_ref[j * FF_CHUNK:(j + 1) * FF_CHUNK, :])
        if j == 0:
            acc_ref[r, :] = part
        else:
            acc_ref[r, :] += part


def _mlp_out(x1_ref, acc_ref, o_ref, mod, nw, r):
    o_ref[r, :] = x1_ref[r, :] + mod[5:6] * _rms(acc_ref[r, :], nw[3:4])


def _log_sigmoid(x):
    return jnp.minimum(x, 0.0) - jnp.log1p(jnp.exp(-jnp.abs(x)))


def _ml_proj_kernel(x_ref, mod_ref, nw_ref, wpt_ref, wp_ref, wg_ref, bg_ref,
                    qt_ref, k_ref, vt_ref, o_ref, ac_ref, gr_ref):
    x = x_ref[...]
    mod = mod_ref[0]
    nw = nw_ref[...]
    tm = x.shape[0]
    h32 = _rms(x, nw[0:1]) * (1.0 + mod[1:2]) + mod[0:1]
    h = h32.astype(bf16)
    nqk = ML_HEADS * ML_DQK

    g = _dot(h, wg_ref[...]) + bg_ref[...]
    ig = g[:, :GATE_LANES]
    lf = _log_sigmoid(g[:, GATE_LANES:])
    pos = lax.broadcasted_iota(jnp.int32, (tm, GATE_LANES), 0) % ML_CHUNK
    is_fwd = lax.broadcasted_iota(jnp.int32, (tm, GATE_LANES), 1) < ML_HEADS

    def chunk_scan(v, op, fill):
        pre, suf = v, v
        step = 1
        while step < ML_CHUNK:
            pre = op(pre, jnp.where(pos >= step, pltpu.roll(pre, step, 0), fill))
            suf = op(suf, jnp.where(pos < ML_CHUNK - step, pltpu.roll(suf, tm - step, 0), fill))
            step *= 2
        return pre, suf

    b_pre, b_suf = chunk_scan(lf, jnp.add, 0.0)
    b = jnp.where(is_fwd, b_pre, b_suf)
    tot = b_pre + b_suf - lf
    a = ig - b
    a_pre, a_suf = chunk_scan(a, jnp.maximum, NEG_BIG)
    cm = jnp.where(is_fwd, a_pre, a_suf)
    am = jnp.maximum(a_pre, a_suf)
    ac_ref[...] = a
    for i, arr in enumerate((a, cm, am, b, tot)):
        arr_t = arr.T
        for ch in range(tm // ML_CHUNK):
            gr_ref[ch, i] = arr_t[0:2 * ML_HEADS, ch * ML_CHUNK:(ch + 1) * ML_CHUNK]

    k_ref[...] = _dot(h, wp_ref[:, 0:nqk]).astype(bf16)
    o_ref[...] = _dot(h, wp_ref[:, nqk:]).astype(bf16)
    h_t = h32.T.astype(bf16)
    qt_ref[...] = _dot(wpt_ref[0:nqk, :], h_t) * (ML_DQK ** -0.5)
    vt_ref[...] = _dot(wpt_ref[nqk:, :], h_t).astype(bf16)


def _ml_proj(x2, mod, nw, w_proj_t, w_proj, w_gate, b_gate, *, tokens_per_mod):
    n, d = x2.shape
    tm = TOKEN_TILE
    assert n % tm == 0 and tokens_per_mod % tm == 0 and tm % ML_CHUNK == 0
    nqk = ML_HEADS * ML_DQK
    nv = ML_HEADS * ML_DV
    tok = lambda width: pl.BlockSpec((tm, width), lambda i: (i, 0))
    tok_t = lambda width: pl.BlockSpec((width, tm), lambda i: (0, i))
    gr_shape = (n // ML_CHUNK, N_GATE_ROWS, 2 * ML_HEADS, ML_CHUNK)
    return pl.pallas_call(
        _ml_proj_kernel,
        grid=(n // tm,),
        in_specs=[tok(d),
                  pl.BlockSpec((1, 6, d), lambda i: (i * tm // tokens_per_mod, 0, 0)),
                  _resident(nw.shape), _resident(w_proj_t.shape), _resident(w_proj.shape),
                  _resident(w_gate.shape), _resident(b_gate.shape)],
        out_specs=[tok_t(nqk), tok(nqk), tok_t(nv), tok(nv), tok(GATE_LANES),
                   pl.BlockSpec((tm // ML_CHUNK,) + gr_shape[1:], lambda i: (i, 0, 0, 0))],
        out_shape=[jax.ShapeDtypeStruct((nqk, n), f32), jax.ShapeDtypeStruct((n, nqk), bf16),
                   jax.ShapeDtypeStruct((nv, n), bf16), jax.ShapeDtypeStruct((n, nv), bf16),
                   jax.ShapeDtypeStruct((n, GATE_LANES), f32), jax.ShapeDtypeStruct(gr_shape, f32)],
        compiler_params=_params("parallel"),
        name="mlstm_proj",
    )(x2, mod, nw, w_proj_t, w_proj, w_gate, b_gate)


def _ml_chunk(qt_ref, k_ref, vt_ref, ac_ref, gr_ref, o_ref, s_ref, m_ref, rhs_ref, ch, reverse):
    L = ML_CHUNK
    rows = slice(ch * L, (ch + 1) * L)
    emit = o_ref is not None
    if emit:
        s_idx = lax.broadcasted_iota(jnp.int32, (L, L), 0)
        t_idx = lax.broadcasted_iota(jnp.int32, (L, L), 1)
        mask = (s_idx >= t_idx) if reverse else (s_idx <= t_idx)
        zero_half = jnp.zeros((ML_DQK, L), bf16)

        def pad_half(q_half, e):
            return jnp.concatenate([q_half, zero_half] if e == 0 else [zero_half, q_half], axis=0)

    first_row = lax.broadcasted_iota(jnp.int32, (STATE_ROWS - ML_DV, L), 0) == 0
    ones_row = jnp.where(first_row, 1.0, 0.0).astype(bf16)
    g0 = ML_HEADS if reverse else 0

    def gate(i, hd):
        return gr_ref[ch, i, g0 + hd:g0 + hd + 1, :]

    def k_pair(pair):
        return k_ref[rows, pair * 2 * ML_DQK:(pair + 1) * 2 * ML_DQK]

    if emit:
        for pair in range(ML_HEADS // 2):
            qt = (qt_ref[(2 * pair) * ML_DQK:(2 * pair + 1) * ML_DQK, rows],
                  qt_ref[(2 * pair + 1) * ML_DQK:(2 * pair + 2) * ML_DQK, rows])
            kq = _dot(k_pair(pair), jnp.concatenate([pad_half(qt[0].astype(bf16), 0),
                                                     pad_half(qt[1].astype(bf16), 1)], axis=1))
            for e in range(2):
                hd = 2 * pair + e
                m_row = m_ref[hd]
                m_run = jnp.maximum(m_row, gate(1, hd))
                a_col = ac_ref[rows, g0 + hd:g0 + hd + 1]
                s_mat = kq[:, e * L:(e + 1) * L] * jnp.exp(jnp.where(mask, a_col - m_run, NEG_BIG))
                rhs_ref[hd, 0:L, :] = s_mat.astype(bf16)
                rhs_ref[hd, L:2 * L, :] = pad_half((qt[e] * jnp.exp(m_row - m_run)).astype(bf16), e)

    for hd in range(ML_HEADS):
        vt_h = vt_ref[hd * ML_DV:(hd + 1) * ML_DV, rows]
        m_row = m_ref[hd]
        state = s_ref[hd]
        if emit:
            lhs = jnp.concatenate([jnp.concatenate([vt_h, ones_row], axis=0), state.astype(bf16)], axis=1)
            res = _dot(lhs, rhs_ref[hd])
            floor = jnp.exp(-(gate(3, hd) + jnp.maximum(m_row, gate(1, hd))))
            inv = 1.0 / jnp.maximum(jnp.abs(res[ML_DV:ML_DV + 1]), floor)
            o_ref[hd * ML_DV:(hd + 1) * ML_DV, rows] = (res[:ML_DV] * inv).astype(o_ref.dtype)
        m_last = jnp.maximum(m_row, gate(2, hd))
        wk = jnp.exp(gate(0, hd) - m_last)
        lhs = jnp.concatenate([vt_h * wk.astype(bf16), jnp.where(first_row, wk, 0.0).astype(bf16)], axis=0)
        kv = _dot(lhs, k_pair(hd // 2))
        s_ref[hd] = jnp.exp(m_row - m_last) * state + kv
        m_ref[hd] = gate(4, hd) + m_last


def _ml_scan_kernel(kc_ref, vtc_ref, grc_ref, qtx_ref, kx_ref, vtx_ref, acx_ref, grx_ref,
                    o_ref, s_ref, m_ref, rhs_ref, *, reverse):
    j = pl.program_id(1)

    def run(qt_ref, k_ref, vt_ref, ac_ref, gr_ref, out_ref):
        nch = k_ref.shape[0] // ML_CHUNK
        order = range(nch - 1, -1, -1) if reverse else range(nch)
        for ch in order:
            _ml_chunk(qt_ref, k_ref, vt_ref, ac_ref, gr_ref, out_ref, s_ref, m_ref, rhs_ref, ch, reverse)

    @pl.when(j == 0)
    def _():
        s_ref[...] = jnp.zeros_like(s_ref)
        m_ref[...] = jnp.zeros_like(m_ref)
        run(None, kc_ref, vtc_ref, None, grc_ref, None)

    @pl.when(j > 0)
    def _():
        run(qtx_ref, kx_ref, vtx_ref, acx_ref, grx_ref, o_ref)


def _ml_scan(ctx_parts, lat_parts, *, batch, reverse):
    kc, vtc, grc = ctx_parts
    qtx, kx, vtx, acx, grx = lat_parts
    t_ctx = kc.shape[0] // batch
    t_lat = kx.shape[0] // batch
    tb = SCAN_BLOCK
    assert t_lat % tb == 0 and tb % ML_CHUNK == 0 and t_ctx % ML_CHUNK == 0
    nblk = t_lat // tb
    nqk = ML_HEADS * ML_DQK
    nv = ML_HEADS * ML_DV
    gr_tail = (N_GATE_ROWS, 2 * ML_HEADS, ML_CHUNK)

    def lat_blk(b, j):
        step = jnp.maximum(j - 1, 0)
        return b * nblk + (nblk - 1 - step if reverse else step)

    lat_tok = lambda width: pl.BlockSpec((tb, width), lambda b, j: (lat_blk(b, j), 0))
    lat_tok_t = lambda width: pl.BlockSpec((width, tb), lambda b, j: (0, lat_blk(b, j)))
    return pl.pallas_call(
        functools.partial(_ml_scan_kernel, reverse=reverse),
        grid=(batch, 1 + nblk),
        in_specs=[pl.BlockSpec((t_ctx, nqk), lambda b, j: (b, 0)),
                  pl.BlockSpec((nv, t_ctx), lambda b, j: (0, b)),
                  pl.BlockSpec((t_ctx // ML_CHUNK,) + gr_tail, lambda b, j: (b, 0, 0, 0)),
                  lat_tok_t(nqk), lat_tok(nqk), lat_tok_t(nv), lat_tok(GATE_LANES),
                  pl.BlockSpec((tb // ML_CHUNK,) + gr_tail, lambda b, j: (lat_blk(b, j), 0, 0, 0))],
        out_specs=lat_tok_t(nv),
        out_shape=jax.ShapeDtypeStruct((nv, batch * t_lat), bf16),
        scratch_shapes=[pltpu.VMEM((ML_HEADS, STATE_ROWS, 2 * ML_DQK), f32),
                        pltpu.VMEM((ML_HEADS, 1, ML_CHUNK), f32),
                        pltpu.VMEM((ML_HEADS, 2 * ML_CHUNK, ML_CHUNK), bf16)],
        compiler_params=_params("parallel", "arbitrary"),
        name="mlstm_scan_bwd" if reverse else "mlstm_scan_fwd",
    )(kc, vtc, grc, qtx, kx, vtx, acx, grx)


def _ml_readout_kernel(hf_ref, hb_ref, og_ref, x_ref, mod_ref, nw_ref, mnw_ref, wout_ref, w1_ref, w2_ref,
                       o_ref, z_ref, h_ref, x1_ref, acc_ref):
    mod = mod_ref[0]
    nw = nw_ref[...]
    mnw = mnw_ref[...]
    parts = _row_parts(x_ref.shape[0], 128)
    for r in parts:
        for hd in range(ML_HEADS):
            vs = slice(hd * ML_DV, (hd + 1) * ML_DV)
            hs = hf_ref[vs, r].astype(f32) + hb_ref[vs, r].astype(f32)
            hn = hs * lax.rsqrt(jnp.mean(hs * hs, axis=0, keepdims=True) + EPS)
            z_ref[r, vs] = (jax.nn.sigmoid(og_ref[r, vs].astype(f32)) * (hn.T * mnw[:, vs])).astype(bf16)
    for r in parts:
        _mixer_residual(x_ref[r, :], _dot(z_ref[r, :], wout_ref[...]), mod, nw, x1_ref, h_ref, r)
    for r in parts:
        _mlp_matmuls(h_ref, acc_ref, w1_ref, w2_ref, r)
    for r in parts:
        _mlp_out(x1_ref, acc_ref, o_ref, mod, nw, r)


def _ml_readout_mlp(hf_t, hb_t, og, x2, mod, nw, ml_norm_w, w_out, w1, w2, *, tokens_per_mod):
    n, d = x2.shape
    tm = TOKEN_TILE
    nv = ML_HEADS * ML_DV
    assert n % tm == 0 and tokens_per_mod % tm == 0
    tok = lambda width: pl.BlockSpec((tm, width), lambda i: (i, 0))
    tok_t = lambda width: pl.BlockSpec((width, tm), lambda i: (0, i))
    return pl.pallas_call(
        _ml_readout_kernel,
        grid=(n // tm,),
        in_specs=[tok_t(nv), tok_t(nv), tok(nv), tok(d),
                  pl.BlockSpec((1, 6, d), lambda i: (i * tm // tokens_per_mod, 0, 0)),
                  _resident(nw.shape), _resident(ml_norm_w.shape), _resident(w_out.shape),
                  _resident(w1.shape), _resident(w2.shape)],
        out_specs=tok(d),
        out_shape=jax.ShapeDtypeStruct((n, d), f32),
        scratch_shapes=[pltpu.VMEM((tm, nv), bf16), pltpu.VMEM((tm, d), bf16),
                        pltpu.VMEM((tm, d), f32), pltpu.VMEM((tm, d), f32)],
        compiler_params=_params("parallel"),
        name="mlstm_readout_mlp",
    )(hf_t, hb_t, og, x2, mod, nw, ml_norm_w, w_out, w1, w2)


def kernel(x, c, ctx, c_ctx, norm_w, mod_w, mod_b, mlp_w1, mlp_w2, conv_w_in, conv_w, conv_w_out,
           ml_w_qkvo, ml_w_if, ml_b_if, ml_norm_w, ml_w_out):
    batch, seq, d = x.shape
    t_ctx = ctx.shape[1]
    depth = norm_w.shape[0]
    assert depth == 2 and seq % GRID_W == 0 and d == ML_HEADS * ML_DV

    pad_rows = -(batch + 1) % 8
    cc = jnp.concatenate([c, c_ctx[None, :], jnp.zeros((pad_rows, d), f32)], axis=0)
    mod = _adaln(cc, mod_w, mod_b).reshape(depth, batch + 1 + pad_rows, 6, d)

    xs = x.reshape(batch * seq, d)
    cs = ctx.reshape(batch * t_ctx, d)

    mx, mc = mod[0, :batch], mod[0, batch:batch + 1]
    w_in = conv_w_in[0].astype(bf16)
    w_out = conv_w_out[0].astype(bf16)
    w1 = mlp_w1[0].astype(bf16)
    w2 = mlp_w2[0].astype(bf16)
    xs = _conv_mixer_mlp(xs, mx, norm_w[0], w_in, conv_w[0], w_out, w1, w2, row_len=GRID_W, tokens_per_mod=seq)
    cs = _conv_mixer_mlp(cs, mc, norm_w[0], w_in, conv_w[0], w_out, w1, w2, row_len=t_ctx,
                         tokens_per_mod=batch * t_ctx)

    mx, mc = mod[1, :batch], mod[1, batch:batch + 1]
    nh = ML_HEADS
    nqk = nh * ML_DQK
    wq, wk, wv, wo = (ml_w_qkvo[0][:, lo:hi] for lo, hi in
                      ((0, nqk), (nqk, 2 * nqk), (2 * nqk, 2 * nqk + d), (2 * nqk + d, 2 * nqk + 2 * d)))
    w_proj_t = jnp.concatenate([wq, wv], axis=1).T.astype(bf16)
    w_proj = jnp.concatenate([wk, wo], axis=1).astype(bf16)
    zpad = jnp.zeros((d, GATE_LANES - 2 * nh), f32)
    w_gate = jnp.concatenate([ml_w_if[0, 0, :, :nh], ml_w_if[0, 1, :, :nh], zpad,
                              ml_w_if[0, 0, :, nh:], ml_w_if[0, 1, :, nh:], zpad], axis=1).astype(bf16)
    bpad = jnp.zeros((GATE_LANES - 2 * nh,), f32)
    b_gate = jnp.concatenate([ml_b_if[0, 0, :nh], ml_b_if[0, 1, :nh], bpad,
                              ml_b_if[0, 0, nh:], ml_b_if[0, 1, nh:], bpad])[None, :]
    qtx, kx, vtx, ogx, acx, grx = _ml_proj(xs, mx, norm_w[1], w_proj_t, w_proj, w_gate, b_gate,
                                           tokens_per_mod=seq)
    _, kc, vtc, _, _, grc = _ml_proj(cs, mc, norm_w[1], w_proj_t, w_proj, w_gate, b_gate,
                                     tokens_per_mod=batch * t_ctx)
    hf_t = _ml_scan((kc, vtc, grc), (qtx, kx, vtx, acx, grx), batch=batch, reverse=False)
    hb_t = _ml_scan((kc, vtc, grc), (qtx, kx, vtx, acx, grx), batch=batch, reverse=True)
    xs = _ml_readout_mlp(hf_t, hb_t, ogx, xs, mx, norm_w[1], ml_norm_w[0][None, :], ml_w_out[0].astype(bf16),
                         mlp_w1[1].astype(bf16), mlp_w2[1].astype(bf16), tokens_per_mod=seq)
    return xs.reshape(batch, seq, d)
```

```python
import functools

import jax
import jax.numpy as jnp
from jax import lax
from jax.experimental import pallas as pl
from jax.experimental.pallas import tpu as pltpu

EPS = 1e-6
GRID_W = 64
ML_HEADS = 8
ML_DV = 128
ML_DQK = 64
ML_CHUNK = 128
GATE_LANES = 128
N_GATE_ROWS = 5
STATE_ROWS = ML_DV + 16
NEG_BIG = -1e30

VMEM_LIMIT_BYTES = 56 * 1024 * 1024
TOKEN_TILE = 512
ROW_PARTS = 2
SCAN_BLOCK = 1024
FF_CHUNK = 1024
CONV_COL_CHUNK = 512
ADALN_COL_TILE = 1536

f32 = jnp.float32
bf16 = jnp.bfloat16


def _dot(a, b):
    return jnp.dot(a, b, preferred_element_type=f32)


def _rms(x, w):
    return x * lax.rsqrt(jnp.mean(x * x, axis=-1, keepdims=True) + EPS) * w


def _resident(shape):
    nd = len(shape)
    return pl.BlockSpec(shape, lambda *_: (0,) * nd, pipeline_mode=pl.Buffered(1))


def _params(*sem):
    return pltpu.CompilerParams(dimension_semantics=sem, vmem_limit_bytes=VMEM_LIMIT_BYTES)


def _adaln_kernel(c_ref, w_ref, b_ref, o_ref):
    c = c_ref[...]
    s = c * jax.nn.sigmoid(c)
    w = w_ref[0]
    sh = s.astype(bf16)
    sl = (s - sh.astype(f32)).astype(bf16)
    wh = w.astype(bf16)
    wl = (w - wh.astype(f32)).astype(bf16)
    o_ref[0] = _dot(sh, wh) + _dot(sh, wl) + _dot(sl, wh) + b_ref[0]


def _adaln(cc, mod_w, mod_b):
    depth, d, n = mod_w.shape
    rows = cc.shape[0]
    tn = ADALN_COL_TILE
    return pl.pallas_call(
        _adaln_kernel,
        grid=(depth, n // tn),
        in_specs=[pl.BlockSpec((rows, d), lambda i, j: (0, 0)),
                  pl.BlockSpec((1, d, tn), lambda i, j: (i, 0, j)),
                  pl.BlockSpec((1, 1, tn), lambda i, j: (i, 0, j))],
        out_specs=pl.BlockSpec((1, rows, tn), lambda i, j: (i, 0, j)),
        out_shape=jax.ShapeDtypeStruct((depth, rows, n), f32),
        compiler_params=_params("arbitrary", "arbitrary"),
        name="adaln",
    )(cc, mod_w, mod_b.reshape(depth, 1, n))


def _conv_mixer_kernel(x_ref, mod_ref, nw_ref, win_ref, cw_ref, wout_ref, w1_ref, w2_ref,
                       o_ref, z_ref, h_ref, x1_ref, acc_ref, *, row_len):
    mod = mod_ref[0]
    nw = nw_ref[...]
    tm, d = x_ref.shape
    cw = cw_ref[...]
    cc = CONV_COL_CHUNK
    parts = _row_parts(tm, row_len)
    pm = parts[0].stop
    pos = lax.broadcasted_iota(jnp.int32, (pm, cc), 0) % row_len

    for r in parts:
        h_ref[r, :] = (_rms(x_ref[r, :], nw[0:1]) * (1.0 + mod[1:2]) + mod[0:1]).astype(bf16)
    for r in parts:
        h = h_ref[r, :]
        for j in range(d // cc):
            cs = slice(j * cc, (j + 1) * cc)
            b_gate = _dot(h, win_ref[:, j * cc:(j + 1) * cc])
            c_gate = _dot(h, win_ref[:, d + j * cc:d + (j + 1) * cc])
            u = c_gate * _dot(h, win_ref[:, 2 * d + j * cc:2 * d + (j + 1) * cc])
            u_prev = jnp.where(pos == 0, 0.0, pltpu.roll(u, 1, 0))
            u_next = jnp.where(pos == row_len - 1, 0.0, pltpu.roll(u, pm - 1, 0))
            y = cw[0:1, cs] * u_prev + cw[1:2, cs] * u + cw[2:3, cs] * u_next
            z_ref[r, cs] = (b_gate * y).astype(bf16)
    for r in parts:
        _mixer_residual(x_ref[r, :], _dot(z_ref[r, :], wout_ref[...]), mod, nw, x1_ref, h_ref, r)
    for r in parts:
        _mlp_matmuls(h_ref, acc_ref, w1_ref, w2_ref, r)
    for r in parts:
        _mlp_out(x1_ref, acc_ref, o_ref, mod, nw, r)


def _conv_mixer_mlp(x2, mod, nw, w_in, conv_w, w_out, w1, w2, *, row_len, tokens_per_mod):
    n, d = x2.shape
    tm = TOKEN_TILE
    assert n % tm == 0 and tm % row_len == 0 and tokens_per_mod % tm == 0
    return pl.pallas_call(
        functools.partial(_conv_mixer_kernel, row_len=row_len),
        grid=(n // tm,),
        in_specs=[pl.BlockSpec((tm, d), lambda i: (i, 0)),
                  pl.BlockSpec((1, 6, d), lambda i: (i * tm // tokens_per_mod, 0, 0)),
                  _resident(nw.shape), _resident(w_in.shape), _resident(conv_w.shape),
                  _resident(w_out.shape), _resident(w1.shape), _resident(w2.shape)],
        out_specs=pl.BlockSpec((tm, d), lambda i: (i, 0)),
        out_shape=jax.ShapeDtypeStruct((n, d), f32),
        scratch_shapes=[pltpu.VMEM((tm, d), bf16), pltpu.VMEM((tm, d), bf16),
                        pltpu.VMEM((tm, d), f32), pltpu.VMEM((tm, d), f32)],
        compiler_params=_params("parallel"),
        name="conv_mixer_mlp",
    )(x2, mod, nw, w_in, conv_w, w_out, w1, w2)


def _row_parts(tm, multiple):
    pm = tm // ROW_PARTS
    assert pm * ROW_PARTS == tm and pm % multiple == 0 and pm % 16 == 0
    return [slice(i * pm, (i + 1) * pm) for i in range(ROW_PARTS)]


def _mixer_residual(x, y_mix, mod, nw, x1_ref, h_ref, r):
    x1 = x + mod[2:3] * _rms(y_mix, nw[1:2])
    x1_ref[r, :] = x1
    h_ref[r, :] = (_rms(x1, nw[2:3]) * (1.0 + mod[4:5]) + mod[3:4]).astype(bf16)


def _mlp_matmuls(h_ref, acc_ref, w1_ref, w2_ref, r):
    ff = w1_ref.shape[1]
    h = h_ref[r, :]
    for j in range(ff // FF_CHUNK):
        a = jnp.maximum(_dot(h, w1_ref[:, j * FF_CHUNK:(j + 1) * FF_CHUNK]), 0.0)
        part = _dot((a * a).astype(bf16), w2_ref[j * FF_CHUNK:(j + 1) * FF_CHUNK, :])
        if j == 0:
            acc_ref[r, :] = part
        else:
            acc_ref[r, :] += part


def _mlp_out(x1_ref, acc_ref, o_ref, mod, nw, r):
    o_ref[r, :] = x1_ref[r, :] + mod[5:6] * _rms(acc_ref[r, :], nw[3:4])


def _log_sigmoid(x):
    return jnp.minimum(x, 0.0) - jnp.log1p(jnp.exp(-jnp.abs(x)))


def _ml_proj_kernel(x_ref, mod_ref, nw_ref, wpt_ref, wp_ref, wg_ref, bg_ref,
                    qt_ref, k_ref, vt_ref, o_ref, ac_ref, gr_ref):
    x = x_ref[...]
    mod = mod_ref[0]
    nw = nw_ref[...]
    tm = x.shape[0]
    h32 = _rms(x, nw[0:1]) * (1.0 + mod[1:2]) + mod[0:1]
    h = h32.astype(bf16)
    nqk = ML_HEADS * ML_DQK

    g = _dot(h, wg_ref[...]) + bg_ref[...]
    ig = g[:, :GATE_LANES]
    lf = _log_sigmoid(g[:, GATE_LANES:])
    pos = lax.broadcasted_iota(jnp.int32, (tm, GATE_LANES), 0) % ML_CHUNK
    is_fwd = lax.broadcasted_iota(jnp.int32, (tm, GATE_LANES), 1) < ML_HEADS

    def chunk_scan(v, op, fill):
        pre, suf = v, v
        step = 1
        while step < ML_CHUNK:
            pre = op(pre, jnp.where(pos >= step, pltpu.roll(pre, step, 0), fill))
            suf = op(suf, jnp.where(pos < ML_CHUNK - step, pltpu.roll(suf, tm - step, 0), fill))
            step *= 2
        return pre, suf

    b_pre, b_suf = chunk_scan(lf, jnp.add, 0.0)
    b = jnp.where(is_fwd, b_pre, b_suf)
    tot = b_pre + b_suf - lf
    a = ig - b
    a_pre, a_suf = chunk_scan(a, jnp.maximum, NEG_BIG)
    cm = jnp.where(is_fwd, a_pre, a_suf)
    am = jnp.maximum(a_pre, a_suf)
    ac_ref[...] = a
    for i, arr in enumerate((a, cm, am, b, tot)):
        arr_t = arr.T
        for ch in range(tm // ML_CHUNK):
            gr_ref[ch, i] = arr_t[0:2 * ML_HEADS, ch * ML_CHUNK:(ch + 1) * ML_CHUNK]

    k_ref[...] = _dot(h, wp_ref[:, 0:nqk]).astype(bf16)
    o_ref[...] = _dot(h, wp_ref[:, nqk:]).astype(bf16)
    h_t = h32.T.astype(bf16)
    qt_ref[...] = _dot(wpt_ref[0:nqk, :], h_t) * (ML_DQK ** -0.5)
    vt_ref[...] = _dot(wpt_ref[nqk:, :], h_t).astype(bf16)


def _ml_proj(x2, mod, nw, w_proj_t, w_proj, w_gate, b_gate, *, tokens_per_mod):
    n, d = x2.shape
    tm = TOKEN_TILE
    assert n % tm == 0 and tokens_per_mod % tm == 0 and tm % ML_CHUNK == 0
    nqk = ML_HEADS * ML_DQK
    nv = ML_HEADS * ML_DV
    tok = lambda width: pl.BlockSpec((tm, width), lambda i: (i, 0))
    tok_t = lambda width: pl.BlockSpec((width, tm), lambda i: (0, i))
    gr_shape = (n // ML_CHUNK, N_GATE_ROWS, 2 * ML_HEADS, ML_CHUNK)
    return pl.pallas_call(
        _ml_proj_kernel,
        grid=(n // tm,),
        in_specs=[tok(d),
                  pl.BlockSpec((1, 6, d), lambda i: (i * tm // tokens_per_mod, 0, 0)),
                  _resident(nw.shape), _resident(w_proj_t.shape), _resident(w_proj.shape),
                  _resident(w_gate.shape), _resident(b_gate.shape)],
        out_specs=[tok_t(nqk), tok(nqk), tok_t(nv), tok(nv), tok(GATE_LANES),
                   pl.BlockSpec((tm // ML_CHUNK,) + gr_shape[1:], lambda i: (i, 0, 0, 0))],
        out_shape=[jax.ShapeDtypeStruct((nqk, n), f32), jax.ShapeDtypeStruct((n, nqk), bf16),
                   jax.ShapeDtypeStruct((nv, n), bf16), jax.ShapeDtypeStruct((n, nv), bf16),
                   jax.ShapeDtypeStruct((n, GATE_LANES), f32), jax.ShapeDtypeStruct(gr_shape, f32)],
        compiler_params=_params("parallel"),
        name="mlstm_proj",
    )(x2, mod, nw, w_proj_t, w_proj, w_gate, b_gate)


def _ml_block(qt_ref, k_ref, vt_ref, ac_ref, gr_ref, o_ref, s_ref, m_ref, rhs_ref, reverse):
    L = ML_CHUNK
    emit = o_ref is not None
    nch = k_ref.shape[0] // L
    order = list(range(nch - 1, -1, -1) if reverse else range(nch))
    if emit:
        s_idx = lax.broadcasted_iota(jnp.int32, (L, L), 0)
        t_idx = lax.broadcasted_iota(jnp.int32, (L, L), 1)
        mask = (s_idx >= t_idx) if reverse else (s_idx <= t_idx)
        zero_half = jnp.zeros((ML_DQK, L), bf16)

        def pad_half(q_half, e):
            return jnp.concatenate([q_half, zero_half] if e == 0 else [zero_half, q_half], axis=0)

    first_row = lax.broadcasted_iota(jnp.int32, (STATE_ROWS - ML_DV, L), 0) == 0
    ones_row = jnp.where(first_row, 1.0, 0.0).astype(bf16)
    g0 = ML_HEADS if reverse else 0

    def gate(ch, i, hd):
        return gr_ref[ch, i, g0 + hd:g0 + hd + 1, :]

    def k_pair(ch, pair):
        return k_ref[ch * L:(ch + 1) * L, pair * 2 * ML_DQK:(pair + 1) * 2 * ML_DQK]

    for i, ch in enumerate(order):
        for hd in range(ML_HEADS):
            m_ref[i + 1, hd] = gate(ch, 4, hd) + jnp.maximum(m_ref[i, hd], gate(ch, 2, hd))

    def stage1(i):
        ch = order[i]
        rows = slice(ch * L, (ch + 1) * L)
        for pair in range(ML_HEADS // 2):
            qt = (qt_ref[(2 * pair) * ML_DQK:(2 * pair + 1) * ML_DQK, rows],
                  qt_ref[(2 * pair + 1) * ML_DQK:(2 * pair + 2) * ML_DQK, rows])
            kq = _dot(k_pair(ch, pair), jnp.concatenate([pad_half(qt[0].astype(bf16), 0),
                                                         pad_half(qt[1].astype(bf16), 1)], axis=1))
            for e in range(2):
                hd = 2 * pair + e
                m_row = m_ref[i, hd]
                m_run = jnp.maximum(m_row, gate(ch, 1, hd))
                a_col = ac_ref[rows, g0 + hd:g0 + hd + 1]
                s_mat = kq[:, e * L:(e + 1) * L] * jnp.exp(jnp.where(mask, a_col - m_run, NEG_BIG))
                rhs_ref[i % 2, hd, 0:L, :] = s_mat.astype(bf16)
                rhs_ref[i % 2, hd, L:2 * L, :] = pad_half((qt[e] * jnp.exp(m_row - m_run)).astype(bf16), e)

    def stage2(i):
        ch = order[i]
        rows = slice(ch * L, (ch + 1) * L)
        for hd in range(ML_HEADS):
            vt_h = vt_ref[hd * ML_DV:(hd + 1) * ML_DV, rows]
            m_row = m_ref[i, hd]
            state = s_ref[hd]
            if emit:
                lhs = jnp.concatenate([jnp.concatenate([vt_h, ones_row], axis=0), state.astype(bf16)], axis=1)
                res = _dot(lhs, rhs_ref[i % 2, hd])
                floor = jnp.exp(-(gate(ch, 3, hd) + jnp.maximum(m_row, gate(ch, 1, hd))))
                inv = 1.0 / jnp.maximum(jnp.abs(res[ML_DV:ML_DV + 1]), floor)
                o_ref[hd * ML_DV:(hd + 1) * ML_DV, rows] = (res[:ML_DV] * inv).astype(o_ref.dtype)
            m_last = jnp.maximum(m_row, gate(ch, 2, hd))
            wk = jnp.exp(gate(ch, 0, hd) - m_last)
            lhs = jnp.concatenate([vt_h * wk.astype(bf16), jnp.where(first_row, wk, 0.0).astype(bf16)], axis=0)
            kv = _dot(lhs, k_pair(ch, hd // 2))
            s_ref[hd] = jnp.exp(m_row - m_last) * state + kv

    if emit:
        stage1(0)
    for i in range(nch):
        if emit and i + 1 < nch:
            stage1(i + 1)
        stage2(i)
    m_ref[0] = m_ref[nch]


def _ml_scan_kernel(kc_ref, vtc_ref, grc_ref, qtx_ref, kx_ref, vtx_ref, acx_ref, grx_ref,
                    o_ref, s_ref, m_ref, rhs_ref, *, reverse):
    j = pl.program_id(1)

    @pl.when(j == 0)
    def _():
        s_ref[...] = jnp.zeros_like(s_ref)
        m_ref[0] = jnp.zeros(m_ref.shape[1:], f32)
        _ml_block(None, kc_ref, vtc_ref, None, grc_ref, None, s_ref, m_ref, rhs_ref, reverse)

    @pl.when(j > 0)
    def _():
        _ml_block(qtx_ref, kx_ref, vtx_ref, acx_ref, grx_ref, o_ref, s_ref, m_ref, rhs_ref, reverse)


def _ml_scan(ctx_parts, lat_parts, *, batch, reverse):
    kc, vtc, grc = ctx_parts
    qtx, kx, vtx, acx, grx = lat_parts
    t_ctx = kc.shape[0] // batch
    t_lat = kx.shape[0] // batch
    tb = SCAN_BLOCK
    assert t_lat % tb == 0 and tb % ML_CHUNK == 0 and t_ctx % ML_CHUNK == 0
    nblk = t_lat // tb
    nqk = ML_HEADS * ML_DQK
    nv = ML_HEADS * ML_DV
    gr_tail = (N_GATE_ROWS, 2 * ML_HEADS, ML_CHUNK)

    def lat_blk(b, j):
        step = jnp.maximum(j - 1, 0)
        return b * nblk + (nblk - 1 - step if reverse else step)

    lat_tok = lambda width: pl.BlockSpec((tb, width), lambda b, j: (lat_blk(b, j), 0))
    lat_tok_t = lambda width: pl.BlockSpec((width, tb), lambda b, j: (0, lat_blk(b, j)))
    return pl.pallas_call(
        functools.partial(_ml_scan_kernel, reverse=reverse),
        grid=(batch, 1 + nblk),
        in_specs=[pl.BlockSpec((t_ctx, nqk), lambda b, j: (b, 0)),
                  pl.BlockSpec((nv, t_ctx), lambda b, j: (0, b)),
                  pl.BlockSpec((t_ctx // ML_CHUNK,) + gr_tail, lambda b, j: (b, 0, 0, 0)),
                  lat_tok_t(nqk), lat_tok(nqk), lat_tok_t(nv), lat_tok(GATE_LANES),
                  pl.BlockSpec((tb // ML_CHUNK,) + gr_tail, lambda b, j: (lat_blk(b, j), 0, 0, 0))],
        out_specs=lat_tok_t(nv),
        out_shape=jax.ShapeDtypeStruct((nv, batch * t_lat), bf16),
        scratch_shapes=[pltpu.VMEM((ML_HEADS, STATE_ROWS, 2 * ML_DQK), f32),
                        pltpu.VMEM((max(tb, t_ctx) // ML_CHUNK + 1, ML_HEADS, 1, ML_CHUNK), f32),
                        pltpu.VMEM((2, ML_HEADS, 2 * ML_CHUNK, ML_CHUNK), bf16)],
        compiler_params=_params("parallel", "arbitrary"),
        name="mlstm_scan_bwd" if reverse else "mlstm_scan_fwd",
    )(kc, vtc, grc, qtx, kx, vtx, acx, grx)


def _ml_readout_kernel(hf_ref, hb_ref, og_ref, x_ref, mod_ref, nw_ref, mnw_ref, wout_ref, w1_ref, w2_ref,
                       o_ref, z_ref, h_ref, x1_ref, acc_ref):
    mod = mod_ref[0]
    nw = nw_ref[...]
    mnw = mnw_ref[...]
    parts = _row_parts(x_ref.shape[0], 128)
    for r in parts:
        for hd in range(ML_HEADS):
            vs = slice(hd * ML_DV, (hd + 1) * ML_DV)
            hs = hf_ref[vs, r].astype(f32) + hb_ref[vs, r].astype(f32)
            hn = hs * lax.rsqrt(jnp.mean(hs * hs, axis=0, keepdims=True) + EPS)
            z_ref[r, vs] = (jax.nn.sigmoid(og_ref[r, vs].astype(f32)) * (hn.T * mnw[:, vs])).astype(bf16)
    for r in parts:
        _mixer_residual(x_ref[r, :], _dot(z_ref[r, :], wout_ref[...]), mod, nw, x1_ref, h_ref, r)
    for r in parts:
        _mlp_matmuls(h_ref, acc_ref, w1_ref, w2_ref, r)
    for r in parts:
        _mlp_out(x1_ref, acc_ref, o_ref, mod, nw, r)


def _ml_readout_mlp(hf_t, hb_t, og, x2, mod, nw, ml_norm_w, w_out, w1, w2, *, tokens_per_mod):
    n, d = x2.shape
    tm = TOKEN_TILE
    nv = ML_HEADS * ML_DV
    assert n % tm == 0 and tokens_per_mod % tm == 0
    tok = lambda width: pl.BlockSpec((tm, width), lambda i: (i, 0))
    tok_t = lambda width: pl.BlockSpec((width, tm), lambda i: (0, i))
    return pl.pallas_call(
        _ml_readout_kernel,
        grid=(n // tm,),
        in_specs=[tok_t(nv), tok_t(nv), tok(nv), tok(d),
                  pl.BlockSpec((1, 6, d), lambda i: (i * tm // tokens_per_mod, 0, 0)),
                  _resident(nw.shape), _resident(ml_norm_w.shape), _resident(w_out.shape),
                  _resident(w1.shape), _resident(w2.shape)],
        out_specs=tok(d),
        out_shape=jax.ShapeDtypeStruct((n, d), f32),
        scratch_shapes=[pltpu.VMEM((tm, nv), bf16), pltpu.VMEM((tm, d), bf16),
                        pltpu.VMEM((tm, d), f32), pltpu.VMEM((tm, d), f32)],
        compiler_params=_params("parallel"),
        name="mlstm_readout_mlp",
    )(hf_t, hb_t, og, x2, mod, nw, ml_norm_w, w_out, w1, w2)


def kernel(x, c, ctx, c_ctx, norm_w, mod_w, mod_b, mlp_w1, mlp_w2, conv_w_in, conv_w, conv_w_out,
           ml_w_qkvo, ml_w_if, ml_b_if, ml_norm_w, ml_w_out):
    batch, seq, d = x.shape
    t_ctx = ctx.shape[1]
    depth = norm_w.shape[0]
    assert depth == 2 and seq % GRID_W == 0 and d == ML_HEADS * ML_DV

    pad_rows = -(batch + 1) % 8
    cc = jnp.concatenate([c, c_ctx[None, :], jnp.zeros((pad_rows, d), f32)], axis=0)
    mod = _adaln(cc, mod_w, mod_b).reshape(depth, batch + 1 + pad_rows, 6, d)

    xs = x.reshape(batch * seq, d)
    cs = ctx.reshape(batch * t_ctx, d)

    mx, mc = mod[0, :batch], mod[0, batch:batch + 1]
    w_in = conv_w_in[0].astype(bf16)
    w_out = conv_w_out[0].astype(bf16)
    w1 = mlp_w1[0].astype(bf16)
    w2 = mlp_w2[0].astype(bf16)
    xs = _conv_mixer_mlp(xs, mx, norm_w[0], w_in, conv_w[0], w_out, w1, w2, row_len=GRID_W, tokens_per_mod=seq)
    cs = _conv_mixer_mlp(cs, mc, norm_w[0], w_in, conv_w[0], w_out, w1, w2, row_len=t_ctx,
                         tokens_per_mod=batch * t_ctx)

    mx, mc = mod[1, :batch], mod[1, batch:batch + 1]
    nh = ML_HEADS
    nqk = nh * ML_DQK
    wq, wk, wv, wo = (ml_w_qkvo[0][:, lo:hi] for lo, hi in
                      ((0, nqk), (nqk, 2 * nqk), (2 * nqk, 2 * nqk + d), (2 * nqk + d, 2 * nqk + 2 * d)))
    w_proj_t = jnp.concatenate([wq, wv], axis=1).T.astype(bf16)
    w_proj = jnp.concatenate([wk, wo], axis=1).astype(bf16)
    zpad = jnp.zeros((d, GATE_LANES - 2 * nh), f32)
    w_gate = jnp.concatenate([ml_w_if[0, 0, :, :nh], ml_w_if[0, 1, :, :nh], zpad,
                              ml_w_if[0, 0, :, nh:], ml_w_if[0, 1, :, nh:], zpad], axis=1).astype(bf16)
    bpad = jnp.zeros((GATE_LANES - 2 * nh,), f32)
    b_gate = jnp.concatenate([ml_b_if[0, 0, :nh], ml_b_if[0, 1, :nh], bpad,
                              ml_b_if[0, 0, nh:], ml_b_if[0, 1, nh:], bpad])[None, :]
    qtx, kx, vtx, ogx, acx, grx = _ml_proj(xs, mx, norm_w[1], w_proj_t, w_proj, w_gate, b_gate,
                                           tokens_per_mod=seq)
    _, kc, vtc, _, _, grc = _ml_proj(cs, mc, norm_w[1], w_proj_t, w_proj, w_gate, b_gate,
                                     tokens_per_mod=batch * t_ctx)
    hf_t = _ml_scan((kc, vtc, grc), (qtx, kx, vtx, acx, grx), batch=batch, reverse=False)
    hb_t = _ml_scan((kc, vtc, grc), (qtx, kx, vtx, acx, grx), batch=batch, reverse=True)
    xs = _ml_readout_mlp(hf_t, hb_t, ogx, xs, mx, norm_w[1], ml_norm_w[0][None, :], ml_w_out[0].astype(bf16),
                         mlp_w1[1].astype(bf16), mlp_w2[1].astype(bf16), tokens_per_mod=seq)
    return xs.reshape(batch, seq, d)
```

```python
import functools

import jax
import jax.numpy as jnp
from jax import lax
from jax.experimental import pallas as pl
from jax.experimental.pallas import tpu as pltpu

EPS = 1e-6
GRID_W = 64
ML_HEADS = 8
ML_DV = 128
ML_DQK = 64
ML_CHUNK = 128
GATE_LANES = 128
N_GATE_ROWS = 5
STATE_ROWS = ML_DV + 16
NEG_BIG = -1e30

VMEM_LIMIT_BYTES = 56 * 1024 * 1024
TOKEN_TILE = 512
ROW_PARTS = 2
SCAN_BLOCK = 1024
FF_CHUNK = 1024
CONV_COL_CHUNK = 512
ADALN_COL_TILE = 1536

f32 = jnp.float32
bf16 = jnp.bfloat16


def _dot(a, b):
    return jnp.dot(a, b, preferred_element_type=f32)


def _rms(x, w):
    return x * lax.rsqrt(jnp.mean(x * x, axis=-1, keepdims=True) + EPS) * w


def _resident(shape):
    nd = len(shape)
    return pl.BlockSpec(shape, lambda *_: (0,) * nd, pipeline_mode=pl.Buffered(1))


def _resident_layer(stacked_shape, layer):
    nd = len(stacked_shape) - 1
    return pl.BlockSpec((None,) + tuple(stacked_shape[1:]), lambda *_: (layer,) + (0,) * nd,
                        pipeline_mode=pl.Buffered(1))


def _params(*sem):
    return pltpu.CompilerParams(dimension_semantics=sem, vmem_limit_bytes=VMEM_LIMIT_BYTES)


def _adaln_kernel(c_ref, w_ref, b_ref, o_ref):
    c = c_ref[...]
    s = c * jax.nn.sigmoid(c)
    w = w_ref[0]
    sh = s.astype(bf16)
    sl = (s - sh.astype(f32)).astype(bf16)
    wh = w.astype(bf16)
    wl = (w - wh.astype(f32)).astype(bf16)
    o_ref[0] = _dot(sh, wh) + _dot(sh, wl) + _dot(sl, wh) + b_ref[0]


def _adaln(cc, mod_w, mod_b):
    depth, d, n = mod_w.shape
    rows = cc.shape[0]
    tn = ADALN_COL_TILE
    return pl.pallas_call(
        _adaln_kernel,
        grid=(depth, n // tn),
        in_specs=[pl.BlockSpec((rows, d), lambda i, j: (0, 0)),
                  pl.BlockSpec((1, d, tn), lambda i, j: (i, 0, j)),
                  pl.BlockSpec((1, 1, tn), lambda i, j: (i, 0, j))],
        out_specs=pl.BlockSpec((1, rows, tn), lambda i, j: (i, 0, j)),
        out_shape=jax.ShapeDtypeStruct((depth, rows, n), f32),
        compiler_params=_params("arbitrary", "arbitrary"),
        name="adaln",
    )(cc, mod_w, mod_b.reshape(depth, 1, n))


def _conv_mixer_kernel(x_ref, mod_ref, nw_ref, win_ref, cw_ref, wout_ref, w1_ref, w2_ref,
                       o_ref, z_ref, h_ref, x1_ref, acc_ref, *, row_len):
    mod = mod_ref[0]
    nw = nw_ref[...]
    tm, d = x_ref.shape
    cw = cw_ref[...]
    cc = CONV_COL_CHUNK
    parts = _row_parts(tm, row_len)

    for r in parts:
        h_ref[r, :] = (_rms(x_ref[r, :], nw[0:1]) * (1.0 + mod[1:2]) + mod[0:1]).astype(bf16)
    for r in parts:
        h = h_ref[r, :]
        pm = r.stop - r.start
        pos = lax.broadcasted_iota(jnp.int32, (pm, cc), 0) % row_len
        for j in range(d // cc):
            cs = slice(j * cc, (j + 1) * cc)
            b_gate = _dot(h, win_ref[:, j * cc:(j + 1) * cc])
            c_gate = _dot(h, win_ref[:, d + j * cc:d + (j + 1) * cc])
            u = c_gate * _dot(h, win_ref[:, 2 * d + j * cc:2 * d + (j + 1) * cc])
            u_prev = jnp.where(pos == 0, 0.0, pltpu.roll(u, 1, 0))
            u_next = jnp.where(pos == row_len - 1, 0.0, pltpu.roll(u, pm - 1, 0))
            y = cw[0:1, cs] * u_prev + cw[1:2, cs] * u + cw[2:3, cs] * u_next
            z_ref[r, cs] = (b_gate * y).astype(bf16)
    for r in parts:
        _mixer_residual(x_ref[r, :], _dot(z_ref[r, :], wout_ref[...]), mod, nw, x1_ref, h_ref, r)
    for r in parts:
        _mlp_matmuls(h_ref, acc_ref, w1_ref, w2_ref, r)
    for r in parts:
        _mlp_out(x1_ref, acc_ref, o_ref, mod, nw, r)


def _conv_mixer_mlp(x2, mod, nw, w_in, conv_w, w_out, w1, w2, *, layer, row_len, tokens_per_mod):
    n, d = x2.shape
    tm = TOKEN_TILE
    assert n % tm == 0 and tm % row_len == 0 and tokens_per_mod % tm == 0
    return pl.pallas_call(
        functools.partial(_conv_mixer_kernel, row_len=row_len),
        grid=(n // tm,),
        in_specs=[pl.BlockSpec((tm, d), lambda i: (i, 0)),
                  pl.BlockSpec((1, 6, d), lambda i: (i * tm // tokens_per_mod, 0, 0)),
                  _resident(nw.shape), _resident(w_in.shape), _resident(conv_w.shape),
                  _resident(w_out.shape), _resident_layer(w1.shape, layer), _resident_layer(w2.shape, layer)],
        out_specs=pl.BlockSpec((tm, d), lambda i: (i, 0)),
        out_shape=jax.ShapeDtypeStruct((n, d), f32),
        scratch_shapes=[pltpu.VMEM((tm, d), bf16), pltpu.VMEM((tm, d), bf16),
                        pltpu.VMEM((tm, d), f32), pltpu.VMEM((tm, d), f32)],
        compiler_params=_params("parallel"),
        name="conv_mixer_mlp",
    )(x2, mod, nw, w_in, conv_w, w_out, w1, w2)


def _row_parts(tm, multiple):
    parts = ROW_PARTS
    while (tm // parts) % multiple:
        parts //= 2
    pm = tm // parts
    assert parts >= 1 and pm * parts == tm and pm % 16 == 0
    return [slice(i * pm, (i + 1) * pm) for i in range(parts)]


def _mixer_residual(x, y_mix, mod, nw, x1_ref, h_ref, r):
    x1 = x + mod[2:3] * _rms(y_mix, nw[1:2])
    x1_ref[r, :] = x1
    h_ref[r, :] = (_rms(x1, nw[2:3]) * (1.0 + mod[4:5]) + mod[3:4]).astype(bf16)


def _mlp_matmuls(h_ref, acc_ref, w1_ref, w2_ref, r):
    ff = w1_ref.shape[1]
    h = h_ref[r, :]
    for j in range(ff // FF_CHUNK):
        a = jnp.maximum(_dot(h, w1_ref[:, j * FF_CHUNK:(j + 1) * FF_CHUNK]), 0.0)
        part = _dot((a * a).astype(bf16), w2_ref[j * FF_CHUNK:(j + 1) * FF_CHUNK, :])
        if j == 0:
            acc_ref[r, :] = part
        else:
            acc_ref[r, :] += part


def _mlp_out(x1_ref, acc_ref, o_ref, mod, nw, r):
    o_ref[r, :] = x1_ref[r, :] + mod[5:6] * _rms(acc_ref[r, :], nw[3:4])


def _log_sigmoid(x):
    return jnp.minimum(x, 0.0) - jnp.log1p(jnp.exp(-jnp.abs(x)))


def _ml_proj_kernel(x_ref, mod_ref, nw_ref, wpt_ref, wp_ref, wg_ref, bg_ref,
                    qt_ref, k_ref, vt_ref, o_ref, ac_ref, gr_ref):
    x = x_ref[...]
    mod = mod_ref[0]
    nw = nw_ref[...]
    tm = x.shape[0]
    h32 = _rms(x, nw[0:1]) * (1.0 + mod[1:2]) + mod[0:1]
    h = h32.astype(bf16)
    nqk = ML_HEADS * ML_DQK

    g = _dot(h, wg_ref[...]) + bg_ref[...]
    ig = g[:, :GATE_LANES]
    lf = _log_sigmoid(g[:, GATE_LANES:])
    pos = lax.broadcasted_iota(jnp.int32, (tm, GATE_LANES), 0) % ML_CHUNK
    is_fwd = lax.broadcasted_iota(jnp.int32, (tm, GATE_LANES), 1) < ML_HEADS

    def chunk_scan(v, op, fill):
        pre, suf = v, v
        step = 1
        while step < ML_CHUNK:
            pre = op(pre, jnp.where(pos >= step, pltpu.roll(pre, step, 0), fill))
            suf = op(suf, jnp.where(pos < ML_CHUNK - step, pltpu.roll(suf, tm - step, 0), fill))
            step *= 2
        return pre, suf

    b_pre, b_suf = chunk_scan(lf, jnp.add, 0.0)
    b = jnp.where(is_fwd, b_pre, b_suf)
    tot = b_pre + b_suf - lf
    a = ig - b
    a_pre, a_suf = chunk_scan(a, jnp.maximum, NEG_BIG)
    cm = jnp.where(is_fwd, a_pre, a_suf)
    am = jnp.maximum(a_pre, a_suf)
    ac_ref[...] = a
    for i, arr in enumerate((a, cm, am, b, tot)):
        arr_t = arr.T
        for ch in range(tm // ML_CHUNK):
            gr_ref[ch, i] = arr_t[0:2 * ML_HEADS, ch * ML_CHUNK:(ch + 1) * ML_CHUNK]

    k_ref[...] = _dot(h, wp_ref[:, 0:nqk]).astype(bf16)
    o_ref[...] = _dot(h, wp_ref[:, nqk:]).astype(bf16)
    h_t = h32.T.astype(bf16)
    qt_ref[...] = _dot(wpt_ref[0:nqk, :], h_t) * (ML_DQK ** -0.5)
    vt_ref[...] = _dot(wpt_ref[nqk:, :], h_t).astype(bf16)


def _ml_proj(x2, mod, nw, w_proj_t, w_proj, w_gate, b_gate, *, tokens_per_mod):
    n, d = x2.shape
    tm = TOKEN_TILE
    assert n % tm == 0 and tokens_per_mod % tm == 0 and tm % ML_CHUNK == 0
    nqk = ML_HEADS * ML_DQK
    nv = ML_HEADS * ML_DV
    tok = lambda width: pl.BlockSpec((tm, width), lambda i: (i, 0))
    tok_t = lambda width: pl.BlockSpec((width, tm), lambda i: (0, i))
    gr_shape = (n // ML_CHUNK, N_GATE_ROWS, 2 * ML_HEADS, ML_CHUNK)
    return pl.pallas_call(
        _ml_proj_kernel,
        grid=(n // tm,),
        in_specs=[tok(d),
                  pl.BlockSpec((1, 6, d), lambda i: (i * tm // tokens_per_mod, 0, 0)),
                  _resident(nw.shape), _resident(w_proj_t.shape), _resident(w_proj.shape),
                  _resident(w_gate.shape), _resident(b_gate.shape)],
        out_specs=[tok_t(nqk), tok(nqk), tok_t(nv), tok(nv), tok(GATE_LANES),
                   pl.BlockSpec((tm // ML_CHUNK,) + gr_shape[1:], lambda i: (i, 0, 0, 0))],
        out_shape=[jax.ShapeDtypeStruct((nqk, n), f32), jax.ShapeDtypeStruct((n, nqk), bf16),
                   jax.ShapeDtypeStruct((nv, n), bf16), jax.ShapeDtypeStruct((n, nv), bf16),
                   jax.ShapeDtypeStruct((n, GATE_LANES), f32), jax.ShapeDtypeStruct(gr_shape, f32)],
        compiler_params=_params("parallel"),
        name="mlstm_proj",
    )(x2, mod, nw, w_proj_t, w_proj, w_gate, b_gate)


def _ml_block(qt_ref, k_ref, vt_ref, ac_ref, gr_ref, o_ref, s_ref, m_ref, rhs_ref, reverse):
    L = ML_CHUNK
    emit = o_ref is not None
    nch = k_ref.shape[0] // L
    order = list(range(nch - 1, -1, -1) if reverse else range(nch))
    if emit:
        s_idx = lax.broadcasted_iota(jnp.int32, (L, L), 0)
        t_idx = lax.broadcasted_iota(jnp.int32, (L, L), 1)
        mask = (s_idx >= t_idx) if reverse else (s_idx <= t_idx)
        zero_half = jnp.zeros((ML_DQK, L), bf16)

        def pad_half(q_half, e):
            return jnp.concatenate([q_half, zero_half] if e == 0 else [zero_half, q_half], axis=0)

    first_row = lax.broadcasted_iota(jnp.int32, (STATE_ROWS - ML_DV, L), 0) == 0
    ones_row = jnp.where(first_row, 1.0, 0.0).astype(bf16)
    g0 = ML_HEADS if reverse else 0

    def gate(ch, i, hd):
        return gr_ref[ch, i, g0 + hd:g0 + hd + 1, :]

    def k_pair(ch, pair):
        return k_ref[ch * L:(ch + 1) * L, pair * 2 * ML_DQK:(pair + 1) * 2 * ML_DQK]

    for i, ch in enumerate(order):
        for hd in range(ML_HEADS):
            m_ref[i + 1, hd] = gate(ch, 4, hd) + jnp.maximum(m_ref[i, hd], gate(ch, 2, hd))

    def stage1(i):
        ch = order[i]
        rows = slice(ch * L, (ch + 1) * L)
        for pair in range(ML_HEADS // 2):
            qt = (qt_ref[(2 * pair) * ML_DQK:(2 * pair + 1) * ML_DQK, rows],
                  qt_ref[(2 * pair + 1) * ML_DQK:(2 * pair + 2) * ML_DQK, rows])
            kq = _dot(k_pair(ch, pair), jnp.concatenate([pad_half(qt[0].astype(bf16), 0),
                                                         pad_half(qt[1].astype(bf16), 1)], axis=1))
            for e in range(2):
                hd = 2 * pair + e
                m_row = m_ref[i, hd]
                m_run = jnp.maximum(m_row, gate(ch, 1, hd))
                a_col = ac_ref[rows, g0 + hd:g0 + hd + 1]
                s_mat = kq[:, e * L:(e + 1) * L] * jnp.exp(jnp.where(mask, a_col - m_run, NEG_BIG))
                rhs_ref[i % 2, hd, 0:L, :] = s_mat.astype(bf16)
                rhs_ref[i % 2, hd, L:2 * L, :] = pad_half((qt[e] * jnp.exp(m_row - m_run)).astype(bf16), e)

    def stage2(i):
        ch = order[i]
        rows = slice(ch * L, (ch + 1) * L)
        for hd in range(ML_HEADS):
            vt_h = vt_ref[hd * ML_DV:(hd + 1) * ML_DV, rows]
            m_row = m_ref[i, hd]
            state = s_ref[hd]
            if emit:
                lhs = jnp.concatenate([jnp.concatenate([vt_h, ones_row], axis=0), state.astype(bf16)], axis=1)
                res = _dot(lhs, rhs_ref[i % 2, hd])
                floor = jnp.exp(-(gate(ch, 3, hd) + jnp.maximum(m_row, gate(ch, 1, hd))))
                inv = 1.0 / jnp.maximum(jnp.abs(res[ML_DV:ML_DV + 1]), floor)
                o_ref[hd * ML_DV:(hd + 1) * ML_DV, rows] = (res[:ML_DV] * inv).astype(o_ref.dtype)
            m_last = jnp.maximum(m_row, gate(ch, 2, hd))
            wk = jnp.exp(gate(ch, 0, hd) - m_last)
            lhs = jnp.concatenate([vt_h * wk.astype(bf16), jnp.where(first_row, wk, 0.0).astype(bf16)], axis=0)
            kv = _dot(lhs, k_pair(ch, hd // 2))
            s_ref[hd] = jnp.exp(m_row - m_last) * state + kv

    if emit:
        stage1(0)
    for i in range(nch):
        if emit and i + 1 < nch:
            stage1(i + 1)
        stage2(i)
    m_ref[0] = m_ref[nch]


def _ml_scan_kernel(kc_ref, vtc_ref, grc_ref, qtx_ref, kx_ref, vtx_ref, acx_ref, grx_ref,
                    o_ref, s_ref, m_ref, rhs_ref, *, reverse):
    j = pl.program_id(1)

    @pl.when(j == 0)
    def _():
        s_ref[...] = jnp.zeros_like(s_ref)
        m_ref[0] = jnp.zeros(m_ref.shape[1:], f32)
        _ml_block(None, kc_ref, vtc_ref, None, grc_ref, None, s_ref, m_ref, rhs_ref, reverse)

    _ml_block(qtx_ref, kx_ref, vtx_ref, acx_ref, grx_ref, o_ref, s_ref, m_ref, rhs_ref, reverse)


def _ml_scan(ctx_parts, lat_parts, *, batch, reverse):
    kc, vtc, grc = ctx_parts
    qtx, kx, vtx, acx, grx = lat_parts
    t_ctx = kc.shape[0] // batch
    t_lat = kx.shape[0] // batch
    tb = SCAN_BLOCK
    assert t_lat % tb == 0 and tb % ML_CHUNK == 0 and t_ctx % ML_CHUNK == 0
    nblk = t_lat // tb
    nqk = ML_HEADS * ML_DQK
    nv = ML_HEADS * ML_DV
    gr_tail = (N_GATE_ROWS, 2 * ML_HEADS, ML_CHUNK)

    def lat_blk(b, j):
        return b * nblk + (nblk - 1 - j if reverse else j)

    lat_tok = lambda width: pl.BlockSpec((tb, width), lambda b, j: (lat_blk(b, j), 0))
    lat_tok_t = lambda width: pl.BlockSpec((width, tb), lambda b, j: (0, lat_blk(b, j)))
    return pl.pallas_call(
        functools.partial(_ml_scan_kernel, reverse=reverse),
        grid=(batch, nblk),
        in_specs=[pl.BlockSpec((t_ctx, nqk), lambda b, j: (b, 0)),
                  pl.BlockSpec((nv, t_ctx), lambda b, j: (0, b)),
                  pl.BlockSpec((t_ctx // ML_CHUNK,) + gr_tail, lambda b, j: (b, 0, 0, 0)),
                  lat_tok_t(nqk), lat_tok(nqk), lat_tok_t(nv), lat_tok(GATE_LANES),
                  pl.BlockSpec((tb // ML_CHUNK,) + gr_tail, lambda b, j: (lat_blk(b, j), 0, 0, 0))],
        out_specs=lat_tok_t(nv),
        out_shape=jax.ShapeDtypeStruct((nv, batch * t_lat), bf16),
        scratch_shapes=[pltpu.VMEM((ML_HEADS, STATE_ROWS, 2 * ML_DQK), f32),
                        pltpu.VMEM((max(tb, t_ctx) // ML_CHUNK + 1, ML_HEADS, 1, ML_CHUNK), f32),
                        pltpu.VMEM((2, ML_HEADS, 2 * ML_CHUNK, ML_CHUNK), bf16)],
        compiler_params=_params("parallel", "arbitrary"),
        name="mlstm_scan_bwd" if reverse else "mlstm_scan_fwd",
    )(kc, vtc, grc, qtx, kx, vtx, acx, grx)


def _ml_readout_kernel(hf_ref, hb_ref, og_ref, x_ref, mod_ref, nw_ref, mnw_ref, wout_ref, w1_ref, w2_ref,
                       o_ref, z_ref, h_ref, x1_ref, acc_ref):
    mod = mod_ref[0]
    nw = nw_ref[...]
    mnw = mnw_ref[...]
    parts = _row_parts(x_ref.shape[0], 128)
    for r in parts:
        for hd in range(ML_HEADS):
            vs = slice(hd * ML_DV, (hd + 1) * ML_DV)
            hs = hf_ref[vs, r].astype(f32) + hb_ref[vs, r].astype(f32)
            hn = hs * lax.rsqrt(jnp.mean(hs * hs, axis=0, keepdims=True) + EPS)
            z_ref[r, vs] = (jax.nn.sigmoid(og_ref[r, vs].astype(f32)) * (hn.T * mnw[:, vs])).astype(bf16)
    for r in parts:
        _mixer_residual(x_ref[r, :], _dot(z_ref[r, :], wout_ref[...]), mod, nw, x1_ref, h_ref, r)
    for r in parts:
        _mlp_matmuls(h_ref, acc_ref, w1_ref, w2_ref, r)
    for r in parts:
        _mlp_out(x1_ref, acc_ref, o_ref, mod, nw, r)


def _ml_readout_mlp(hf_t, hb_t, og, x2, mod, nw, ml_norm_w, w_out, w1, w2, *, layer, tokens_per_mod):
    n, d = x2.shape
    tm = TOKEN_TILE
    nv = ML_HEADS * ML_DV
    assert n % tm == 0 and tokens_per_mod % tm == 0
    tok = lambda width: pl.BlockSpec((tm, width), lambda i: (i, 0))
    tok_t = lambda width: pl.BlockSpec((width, tm), lambda i: (0, i))
    return pl.pallas_call(
        _ml_readout_kernel,
        grid=(n // tm,),
        in_specs=[tok_t(nv), tok_t(nv), tok(nv), tok(d),
                  pl.BlockSpec((1, 6, d), lambda i: (i * tm // tokens_per_mod, 0, 0)),
                  _resident(nw.shape), _resident(ml_norm_w.shape), _resident(w_out.shape),
                  _resident_layer(w1.shape, layer), _resident_layer(w2.shape, layer)],
        out_specs=tok(d),
        out_shape=jax.ShapeDtypeStruct((n, d), f32),
        scratch_shapes=[pltpu.VMEM((tm, nv), bf16), pltpu.VMEM((tm, d), bf16),
                        pltpu.VMEM((tm, d), f32), pltpu.VMEM((tm, d), f32)],
        compiler_params=_params("parallel"),
        name="mlstm_readout_mlp",
    )(hf_t, hb_t, og, x2, mod, nw, ml_norm_w, w_out, w1, w2)


def kernel(x, c, ctx, c_ctx, norm_w, mod_w, mod_b, mlp_w1, mlp_w2, conv_w_in, conv_w, conv_w_out,
           ml_w_qkvo, ml_w_if, ml_b_if, ml_norm_w, ml_w_out):
    batch, seq, d = x.shape
    t_ctx = ctx.shape[1]
    depth = norm_w.shape[0]
    assert depth == 2 and seq % GRID_W == 0 and d == ML_HEADS * ML_DV

    pad_rows = -(batch + 1) % 8
    cc = jnp.concatenate([c, c_ctx[None, :], jnp.zeros((pad_rows, d), f32)], axis=0)
    mod = _adaln(cc, mod_w, mod_b).reshape(depth, batch + 1 + pad_rows, 6, d)

    xs = x.reshape(batch * seq, d)
    cs = ctx.reshape(batch * t_ctx, d)

    mx, mc = mod[0, :batch], mod[0, batch:batch + 1]
    w_in = conv_w_in[0].astype(bf16)
    w_out = conv_w_out[0].astype(bf16)
    w1 = mlp_w1.astype(bf16)
    w2 = mlp_w2.astype(bf16)
    xs = _conv_mixer_mlp(xs, mx, norm_w[0], w_in, conv_w[0], w_out, w1, w2, layer=0, row_len=GRID_W,
                         tokens_per_mod=seq)
    cs = _conv_mixer_mlp(cs, mc, norm_w[0], w_in, conv_w[0], w_out, w1, w2, layer=0, row_len=t_ctx,
                         tokens_per_mod=batch * t_ctx)

    mx, mc = mod[1, :batch], mod[1, batch:batch + 1]
    nh = ML_HEADS
    nqk = nh * ML_DQK
    wq, wk, wv, wo = (ml_w_qkvo[0][:, lo:hi] for lo, hi in
                      ((0, nqk), (nqk, 2 * nqk), (2 * nqk, 2 * nqk + d), (2 * nqk + d, 2 * nqk + 2 * d)))
    w_proj_t = jnp.concatenate([wq, wv], axis=1).T.astype(bf16)
    w_proj = jnp.concatenate([wk, wo], axis=1).astype(bf16)
    zpad = jnp.zeros((d, GATE_LANES - 2 * nh), f32)
    w_gate = jnp.concatenate([ml_w_if[0, 0, :, :nh], ml_w_if[0, 1, :, :nh], zpad,
                              ml_w_if[0, 0, :, nh:], ml_w_if[0, 1, :, nh:], zpad], axis=1).astype(bf16)
    bpad = jnp.zeros((GATE_LANES - 2 * nh,), f32)
    b_gate = jnp.concatenate([ml_b_if[0, 0, :nh], ml_b_if[0, 1, :nh], bpad,
                              ml_b_if[0, 0, nh:], ml_b_if[0, 1, nh:], bpad])[None, :]
    qtx, kx, vtx, ogx, acx, grx = _ml_proj(xs, mx, norm_w[1], w_proj_t, w_proj, w_gate, b_gate,
                                           tokens_per_mod=seq)
    _, kc, vtc, _, _, grc = _ml_proj(cs, mc, norm_w[1], w_proj_t, w_proj, w_gate, b_gate,
                                     tokens_per_mod=batch * t_ctx)
    hf_t = _ml_scan((kc, vtc, grc), (qtx, kx, vtx, acx, grx), batch=batch, reverse=False)
    hb_t = _ml_scan((kc, vtc, grc), (qtx, kx, vtx, acx, grx), batch=batch, reverse=True)
    xs = _ml_readout_mlp(hf_t, hb_t, ogx, xs, mx, norm_w[1], ml_norm_w[0][None, :], ml_w_out[0].astype(bf16),
                         w1, w2, layer=1, tokens_per_mod=seq)
    return xs.reshape(batch, seq, d)
```

```python
import functools

import jax
import jax.numpy as jnp
from jax import lax
from jax.experimental import pallas as pl
from jax.experimental.pallas import tpu as pltpu

EPS = 1e-6
GRID_W = 64
ML_HEADS = 8
ML_DV = 128
ML_DQK = 64
ML_CHUNK = 128
GATE_LANES = 128
N_GATE_ROWS = 5
STATE_ROWS = ML_DV + 16
NEG_BIG = -1e30

VMEM_LIMIT_BYTES = 56 * 1024 * 1024
TOKEN_TILE = 512
ROW_PARTS = 2
SCAN_BLOCK = 2048
FF_CHUNK = 1024
CONV_COL_CHUNK = 512
ADALN_COL_TILE = 1536

f32 = jnp.float32
bf16 = jnp.bfloat16


def _dot(a, b):
    return jnp.dot(a, b, preferred_element_type=f32)


def _rms(x, w):
    return x * lax.rsqrt(jnp.mean(x * x, axis=-1, keepdims=True) + EPS) * w


def _resident(shape):
    nd = len(shape)
    return pl.BlockSpec(shape, lambda *_: (0,) * nd, pipeline_mode=pl.Buffered(1))


def _resident_layer(stacked_shape, layer):
    nd = len(stacked_shape) - 1
    return pl.BlockSpec((None,) + tuple(stacked_shape[1:]), lambda *_: (layer,) + (0,) * nd,
                        pipeline_mode=pl.Buffered(1))


def _params(*sem):
    return pltpu.CompilerParams(dimension_semantics=sem, vmem_limit_bytes=VMEM_LIMIT_BYTES)


def _adaln_kernel(c_ref, w_ref, b_ref, o_ref):
    c = c_ref[...]
    s = c * jax.nn.sigmoid(c)
    w = w_ref[0]
    sh = s.astype(bf16)
    sl = (s - sh.astype(f32)).astype(bf16)
    wh = w.astype(bf16)
    wl = (w - wh.astype(f32)).astype(bf16)
    o_ref[0] = _dot(sh, wh) + _dot(sh, wl) + _dot(sl, wh) + b_ref[0]


def _adaln(cc, mod_w, mod_b):
    depth, d, n = mod_w.shape
    rows = cc.shape[0]
    tn = ADALN_COL_TILE
    return pl.pallas_call(
        _adaln_kernel,
        grid=(depth, n // tn),
        in_specs=[pl.BlockSpec((rows, d), lambda i, j: (0, 0)),
                  pl.BlockSpec((1, d, tn), lambda i, j: (i, 0, j)),
                  pl.BlockSpec((1, 1, tn), lambda i, j: (i, 0, j))],
        out_specs=pl.BlockSpec((1, rows, tn), lambda i, j: (i, 0, j)),
        out_shape=jax.ShapeDtypeStruct((depth, rows, n), f32),
        compiler_params=_params("arbitrary", "arbitrary"),
        name="adaln",
    )(cc, mod_w, mod_b.reshape(depth, 1, n))


def _conv_mixer_kernel(x_ref, mod_ref, nw_ref, win_ref, cw_ref, wout_ref, w1_ref, w2_ref,
                       o_ref, z_ref, h_ref, x1_ref, acc_ref, *, row_len):
    mod = mod_ref[0]
    nw = nw_ref[...]
    tm, d = x_ref.shape
    cw = cw_ref[...]
    cc = CONV_COL_CHUNK
    parts = _row_parts(tm, row_len)

    for r in parts:
        h_ref[r, :] = (_rms(x_ref[r, :], nw[0:1]) * (1.0 + mod[1:2]) + mod[0:1]).astype(bf16)
    for r in parts:
        h = h_ref[r, :]
        pm = r.stop - r.start
        pos = lax.broadcasted_iota(jnp.int32, (pm, cc), 0) % row_len
        for j in range(d // cc):
            cs = slice(j * cc, (j + 1) * cc)
            b_gate = _dot(h, win_ref[:, j * cc:(j + 1) * cc])
            c_gate = _dot(h, win_ref[:, d + j * cc:d + (j + 1) * cc])
            u = c_gate * _dot(h, win_ref[:, 2 * d + j * cc:2 * d + (j + 1) * cc])
            u_prev = jnp.where(pos == 0, 0.0, pltpu.roll(u, 1, 0))
            u_next = jnp.where(pos == row_len - 1, 0.0, pltpu.roll(u, pm - 1, 0))
            y = cw[0:1, cs] * u_prev + cw[1:2, cs] * u + cw[2:3, cs] * u_next
            z_ref[r, cs] = (b_gate * y).astype(bf16)
    for r in parts:
        _mixer_residual(x_ref[r, :], _dot(z_ref[r, :], wout_ref[...]), mod, nw, x1_ref, h_ref, r)
    for r in parts:
        _mlp_matmuls(h_ref, acc_ref, w1_ref, w2_ref, r)
    for r in parts:
        _mlp_out(x1_ref, acc_ref, o_ref, mod, nw, r)


def _conv_mixer_mlp(x2, mod, nw, w_in, conv_w, w_out, w1, w2, *, layer, row_len, tokens_per_mod):
    n, d = x2.shape
    tm = TOKEN_TILE
    assert n % tm == 0 and tm % row_len == 0 and tokens_per_mod % tm == 0
    return pl.pallas_call(
        functools.partial(_conv_mixer_kernel, row_len=row_len),
        grid=(n // tm,),
        in_specs=[pl.BlockSpec((tm, d), lambda i: (i, 0)),
                  pl.BlockSpec((1, 6, d), lambda i: (i * tm // tokens_per_mod, 0, 0)),
                  _resident(nw.shape), _resident(w_in.shape), _resident(conv_w.shape),
                  _resident(w_out.shape), _resident_layer(w1.shape, layer), _resident_layer(w2.shape, layer)],
        out_specs=pl.BlockSpec((tm, d), lambda i: (i, 0)),
        out_shape=jax.ShapeDtypeStruct((n, d), f32),
        scratch_shapes=[pltpu.VMEM((tm, d), bf16), pltpu.VMEM((tm, d), bf16),
                        pltpu.VMEM((tm, d), f32), pltpu.VMEM((tm, d), f32)],
        compiler_params=_params("parallel"),
        name="conv_mixer_mlp",
    )(x2, mod, nw, w_in, conv_w, w_out, w1, w2)


def _row_parts(tm, multiple):
    parts = ROW_PARTS
    while (tm // parts) % multiple:
        parts //= 2
    pm = tm // parts
    assert parts >= 1 and pm * parts == tm and pm % 16 == 0
    return [slice(i * pm, (i + 1) * pm) for i in range(parts)]


def _mixer_residual(x, y_mix, mod, nw, x1_ref, h_ref, r):
    x1 = x + mod[2:3] * _rms(y_mix, nw[1:2])
    x1_ref[r, :] = x1
    h_ref[r, :] = (_rms(x1, nw[2:3]) * (1.0 + mod[4:5]) + mod[3:4]).astype(bf16)


def _mlp_matmuls(h_ref, acc_ref, w1_ref, w2_ref, r):
    ff = w1_ref.shape[1]
    h = h_ref[r, :]
    for j in range(ff // FF_CHUNK):
        a = jnp.maximum(_dot(h, w1_ref[:, j * FF_CHUNK:(j + 1) * FF_CHUNK]), 0.0)
        part = _dot((a * a).astype(bf16), w2_ref[j * FF_CHUNK:(j + 1) * FF_CHUNK, :])
        if j == 0:
            acc_ref[r, :] = part
        else:
            acc_ref[r, :] += part


def _mlp_out(x1_ref, acc_ref, o_ref, mod, nw, r):
    o_ref[r, :] = x1_ref[r, :] + mod[5:6] * _rms(acc_ref[r, :], nw[3:4])


def _log_sigmoid(x):
    return jnp.minimum(x, 0.0) - jnp.log1p(jnp.exp(-jnp.abs(x)))


def _ml_proj_kernel(x_ref, mod_ref, nw_ref, wpt_ref, wp_ref, wg_ref, bg_ref,
                    qt_ref, k_ref, vt_ref, o_ref, ac_ref, gr_ref):
    mod = mod_ref[0]
    nw = nw_ref[...]
    nqk = ML_HEADS * ML_DQK
    h32 = _rms(x_ref[...], nw[0:1]) * (1.0 + mod[1:2]) + mod[0:1]
    h = h32.astype(bf16)
    _gate_terms(_dot(h, wg_ref[...]) + bg_ref[...], ac_ref, gr_ref, slice(0, x_ref.shape[0]))
    k_ref[...] = _dot(h, wp_ref[:, 0:nqk]).astype(bf16)
    o_ref[...] = _dot(h, wp_ref[:, nqk:]).astype(bf16)
    h_t = h32.T.astype(bf16)
    qt_ref[...] = _dot(wpt_ref[0:nqk, :], h_t) * (ML_DQK ** -0.5)
    vt_ref[...] = _dot(wpt_ref[nqk:, :], h_t).astype(bf16)


def _gate_terms(g, ac_ref, gr_ref, r):
    ng = 2 * ML_HEADS
    n = r.stop - r.start
    ig = g[:, :GATE_LANES].T[0:ng]
    lf = _log_sigmoid(g[:, GATE_LANES:].T[0:ng])
    pos = lax.broadcasted_iota(jnp.int32, (ng, n), 1) % ML_CHUNK
    is_fwd = lax.broadcasted_iota(jnp.int32, (ng, n), 0) < ML_HEADS

    def chunk_scan(v, op, fill):
        pre, suf = v, v
        step = 1
        while step < ML_CHUNK:
            pre = op(pre, jnp.where(pos >= step, pltpu.roll(pre, step, 1), fill))
            suf = op(suf, jnp.where(pos < ML_CHUNK - step, pltpu.roll(suf, n - step, 1), fill))
            step *= 2
        return pre, suf

    b_pre, b_suf = chunk_scan(lf, jnp.add, 0.0)
    b = jnp.where(is_fwd, b_pre, b_suf)
    tot = b_pre + b_suf - lf
    a = ig - b
    a_pre, a_suf = chunk_scan(a, jnp.maximum, NEG_BIG)
    cm = jnp.where(is_fwd, a_pre, a_suf)
    am = jnp.maximum(a_pre, a_suf)
    for i, arr in enumerate((a, cm, am, b, tot)):
        for ch in range(n // ML_CHUNK):
            gr_ref[r.start // ML_CHUNK + ch, i] = arr[:, ch * ML_CHUNK:(ch + 1) * ML_CHUNK]
    ac_ref[r, :] = jnp.concatenate([a, jnp.zeros((GATE_LANES - ng, n), f32)], axis=0).T


def _ml_proj(x2, mod, nw, w_proj_t, w_proj, w_gate, b_gate, *, tokens_per_mod):
    n, d = x2.shape
    tm = TOKEN_TILE
    assert n % tm == 0 and tokens_per_mod % tm == 0 and tm % ML_CHUNK == 0
    nqk = ML_HEADS * ML_DQK
    nv = ML_HEADS * ML_DV
    tok = lambda width: pl.BlockSpec((tm, width), lambda i: (i, 0))
    tok_t = lambda width: pl.BlockSpec((width, tm), lambda i: (0, i))
    gr_shape = (n // ML_CHUNK, N_GATE_ROWS, 2 * ML_HEADS, ML_CHUNK)
    return pl.pallas_call(
        _ml_proj_kernel,
        grid=(n // tm,),
        in_specs=[tok(d),
                  pl.BlockSpec((1, 6, d), lambda i: (i * tm // tokens_per_mod, 0, 0)),
                  _resident(nw.shape), _resident(w_proj_t.shape), _resident(w_proj.shape),
                  _resident(w_gate.shape), _resident(b_gate.shape)],
        out_specs=[tok_t(nqk), tok(nqk), tok_t(nv), tok(nv), tok(GATE_LANES),
                   pl.BlockSpec((tm // ML_CHUNK,) + gr_shape[1:], lambda i: (i, 0, 0, 0))],
        out_shape=[jax.ShapeDtypeStruct((nqk, n), f32), jax.ShapeDtypeStruct((n, nqk), bf16),
                   jax.ShapeDtypeStruct((nv, n), bf16), jax.ShapeDtypeStruct((n, nv), bf16),
                   jax.ShapeDtypeStruct((n, GATE_LANES), f32), jax.ShapeDtypeStruct(gr_shape, f32)],
        compiler_params=_params("parallel"),
        name="mlstm_proj",
    )(x2, mod, nw, w_proj_t, w_proj, w_gate, b_gate)


def _ml_block(qt_ref, k_ref, vt_ref, ac_ref, gr_ref, o_ref, s_ref, m_ref, rhs_ref, reverse):
    L = ML_CHUNK
    emit = o_ref is not None
    nch = k_ref.shape[0] // L
    order = list(range(nch - 1, -1, -1) if reverse else range(nch))
    if emit:
        s_idx = lax.broadcasted_iota(jnp.int32, (L, L), 0)
        t_idx = lax.broadcasted_iota(jnp.int32, (L, L), 1)
        mask = (s_idx >= t_idx) if reverse else (s_idx <= t_idx)
        zero_half = jnp.zeros((ML_DQK, L), bf16)

        def pad_half(q_half, e):
            return jnp.concatenate([q_half, zero_half] if e == 0 else [zero_half, q_half], axis=0)

    first_row = lax.broadcasted_iota(jnp.int32, (STATE_ROWS - ML_DV, L), 0) == 0
    ones_row = jnp.where(first_row, 1.0, 0.0).astype(bf16)
    g0 = ML_HEADS if reverse else 0

    def gate(ch, i, hd):
        return gr_ref[ch, i, g0 + hd:g0 + hd + 1, :]

    def k_pair(ch, pair):
        return k_ref[ch * L:(ch + 1) * L, pair * 2 * ML_DQK:(pair + 1) * 2 * ML_DQK]

    for i, ch in enumerate(order):
        for hd in range(ML_HEADS):
            m_ref[i + 1, hd] = gate(ch, 4, hd) + jnp.maximum(m_ref[i, hd], gate(ch, 2, hd))

    def stage1(i):
        ch = order[i]
        rows = slice(ch * L, (ch + 1) * L)
        for pair in range(ML_HEADS // 2):
            qt = (qt_ref[(2 * pair) * ML_DQK:(2 * pair + 1) * ML_DQK, rows],
                  qt_ref[(2 * pair + 1) * ML_DQK:(2 * pair + 2) * ML_DQK, rows])
            kq = _dot(k_pair(ch, pair), jnp.concatenate([pad_half(qt[0].astype(bf16), 0),
                                                         pad_half(qt[1].astype(bf16), 1)], axis=1))
            for e in range(2):
                hd = 2 * pair + e
                m_row = m_ref[i, hd]
                m_run = jnp.maximum(m_row, gate(ch, 1, hd))
                a_col = ac_ref[rows, g0 + hd:g0 + hd + 1]
                s_mat = kq[:, e * L:(e + 1) * L] * jnp.exp(jnp.where(mask, a_col - m_run, NEG_BIG))
                rhs_ref[i % 2, hd, 0:L, :] = s_mat.astype(bf16)
                rhs_ref[i % 2, hd, L:2 * L, :] = pad_half((qt[e] * jnp.exp(m_row - m_run)).astype(bf16), e)

    def stage2(i):
        ch = order[i]
        rows = slice(ch * L, (ch + 1) * L)
        for hd in range(ML_HEADS):
            vt_h = vt_ref[hd * ML_DV:(hd + 1) * ML_DV, rows]
            m_row = m_ref[i, hd]
            state = s_ref[hd]
            if emit:
                lhs = jnp.concatenate([jnp.concatenate([vt_h, ones_row], axis=0), state.astype(bf16)], axis=1)
                res = _dot(lhs, rhs_ref[i % 2, hd])
                floor = jnp.exp(-(gate(ch, 3, hd) + jnp.maximum(m_row, gate(ch, 1, hd))))
                inv = 1.0 / jnp.maximum(jnp.abs(res[ML_DV:ML_DV + 1]), floor)
                o_ref[hd * ML_DV:(hd + 1) * ML_DV, rows] = (res[:ML_DV] * inv).astype(o_ref.dtype)
            m_last = jnp.maximum(m_row, gate(ch, 2, hd))
            wk = jnp.exp(gate(ch, 0, hd) - m_last)
            lhs = jnp.concatenate([vt_h * wk.astype(bf16), jnp.where(first_row, wk, 0.0).astype(bf16)], axis=0)
            kv = _dot(lhs, k_pair(ch, hd // 2))
            s_ref[hd] = jnp.exp(m_row - m_last) * state + kv

    if emit:
        stage1(0)
    for i in range(nch):
        if emit and i + 1 < nch:
            stage1(i + 1)
        stage2(i)
    m_ref[0] = m_ref[nch]


def _ml_scan_kernel(kc_ref, vtc_ref, grc_ref, qtx_ref, kx_ref, vtx_ref, acx_ref, grx_ref,
                    o_ref, s_ref, m_ref, rhs_ref, *, reverse):
    j = pl.program_id(1)

    @pl.when(j == 0)
    def _():
        s_ref[...] = jnp.zeros_like(s_ref)
        m_ref[0] = jnp.zeros(m_ref.shape[1:], f32)
        _ml_block(None, kc_ref, vtc_ref, None, grc_ref, None, s_ref, m_ref, rhs_ref, reverse)

    _ml_block(qtx_ref, kx_ref, vtx_ref, acx_ref, grx_ref, o_ref, s_ref, m_ref, rhs_ref, reverse)


def _ml_scan(ctx_parts, lat_parts, *, batch, reverse):
    kc, vtc, grc = ctx_parts
    qtx, kx, vtx, acx, grx = lat_parts
    t_ctx = kc.shape[0] // batch
    t_lat = kx.shape[0] // batch
    tb = SCAN_BLOCK
    assert t_lat % tb == 0 and tb % ML_CHUNK == 0 and t_ctx % ML_CHUNK == 0
    nblk = t_lat // tb
    nqk = ML_HEADS * ML_DQK
    nv = ML_HEADS * ML_DV
    gr_tail = (N_GATE_ROWS, 2 * ML_HEADS, ML_CHUNK)

    def lat_blk(b, j):
        return b * nblk + (nblk - 1 - j if reverse else j)

    lat_tok = lambda width: pl.BlockSpec((tb, width), lambda b, j: (lat_blk(b, j), 0))
    lat_tok_t = lambda width: pl.BlockSpec((width, tb), lambda b, j: (0, lat_blk(b, j)))
    return pl.pallas_call(
        functools.partial(_ml_scan_kernel, reverse=reverse),
        grid=(batch, nblk),
        in_specs=[pl.BlockSpec((t_ctx, nqk), lambda b, j: (b, 0)),
                  pl.BlockSpec((nv, t_ctx), lambda b, j: (0, b)),
                  pl.BlockSpec((t_ctx // ML_CHUNK,) + gr_tail, lambda b, j: (b, 0, 0, 0)),
                  lat_tok_t(nqk), lat_tok(nqk), lat_tok_t(nv), lat_tok(GATE_LANES),
                  pl.BlockSpec((tb // ML_CHUNK,) + gr_tail, lambda b, j: (lat_blk(b, j), 0, 0, 0))],
        out_specs=lat_tok_t(nv),
        out_shape=jax.ShapeDtypeStruct((nv, batch * t_lat), bf16),
        scratch_shapes=[pltpu.VMEM((ML_HEADS, STATE_ROWS, 2 * ML_DQK), f32),
                        pltpu.VMEM((max(tb, t_ctx) // ML_CHUNK + 1, ML_HEADS, 1, ML_CHUNK), f32),
                        pltpu.VMEM((2, ML_HEADS, 2 * ML_CHUNK, ML_CHUNK), bf16)],
        compiler_params=_params("parallel", "arbitrary"),
        name="mlstm_scan_bwd" if reverse else "mlstm_scan_fwd",
    )(kc, vtc, grc, qtx, kx, vtx, acx, grx)


def _ml_readout_kernel(hf_ref, hb_ref, og_ref, x_ref, mod_ref, nw_ref, mnw_ref, wout_ref, w1_ref, w2_ref,
                       o_ref, z_ref, h_ref, x1_ref, acc_ref):
    mod = mod_ref[0]
    nw = nw_ref[...]
    mnw = mnw_ref[...]
    parts = _row_parts(x_ref.shape[0], 128)
    for r in parts:
        for hd in range(ML_HEADS):
            vs = slice(hd * ML_DV, (hd + 1) * ML_DV)
            hs = hf_ref[vs, r].astype(f32) + hb_ref[vs, r].astype(f32)
            hn = hs * lax.rsqrt(jnp.mean(hs * hs, axis=0, keepdims=True) + EPS)
            z_ref[r, vs] = (jax.nn.sigmoid(og_ref[r, vs].astype(f32)) * (hn.T * mnw[:, vs])).astype(bf16)
    for r in parts:
        _mixer_residual(x_ref[r, :], _dot(z_ref[r, :], wout_ref[...]), mod, nw, x1_ref, h_ref, r)
    for r in parts:
        _mlp_matmuls(h_ref, acc_ref, w1_ref, w2_ref, r)
    for r in parts:
        _mlp_out(x1_ref, acc_ref, o_ref, mod, nw, r)


def _ml_readout_mlp(hf_t, hb_t, og, x2, mod, nw, ml_norm_w, w_out, w1, w2, *, layer, tokens_per_mod):
    n, d = x2.shape
    tm = TOKEN_TILE
    nv = ML_HEADS * ML_DV
    assert n % tm == 0 and tokens_per_mod % tm == 0
    tok = lambda width: pl.BlockSpec((tm, width), lambda i: (i, 0))
    tok_t = lambda width: pl.BlockSpec((width, tm), lambda i: (0, i))
    return pl.pallas_call(
        _ml_readout_kernel,
        grid=(n // tm,),
        in_specs=[tok_t(nv), tok_t(nv), tok(nv), tok(d),
                  pl.BlockSpec((1, 6, d), lambda i: (i * tm // tokens_per_mod, 0, 0)),
                  _resident(nw.shape), _resident(ml_norm_w.shape), _resident(w_out.shape),
                  _resident_layer(w1.shape, layer), _resident_layer(w2.shape, layer)],
        out_specs=tok(d),
        out_shape=jax.ShapeDtypeStruct((n, d), f32),
        scratch_shapes=[pltpu.VMEM((tm, nv), bf16), pltpu.VMEM((tm, d), bf16),
                        pltpu.VMEM((tm, d), f32), pltpu.VMEM((tm, d), f32)],
        compiler_params=_params("parallel"),
        name="mlstm_readout_mlp",
    )(hf_t, hb_t, og, x2, mod, nw, ml_norm_w, w_out, w1, w2)


def kernel(x, c, ctx, c_ctx, norm_w, mod_w, mod_b, mlp_w1, mlp_w2, conv_w_in, conv_w, conv_w_out,
           ml_w_qkvo, ml_w_if, ml_b_if, ml_norm_w, ml_w_out):
    batch, seq, d = x.shape
    t_ctx = ctx.shape[1]
    depth = norm_w.shape[0]
    assert depth == 2 and seq % GRID_W == 0 and d == ML_HEADS * ML_DV

    pad_rows = -(batch + 1) % 8
    cc = jnp.concatenate([c, c_ctx[None, :], jnp.zeros((pad_rows, d), f32)], axis=0)
    mod = _adaln(cc, mod_w, mod_b).reshape(depth, batch + 1 + pad_rows, 6, d)

    xs = x.reshape(batch * seq, d)
    cs = ctx.reshape(batch * t_ctx, d)

    mx, mc = mod[0, :batch], mod[0, batch:batch + 1]
    w_in = conv_w_in[0].astype(bf16)
    w_out = conv_w_out[0].astype(bf16)
    w1 = mlp_w1.astype(bf16)
    w2 = mlp_w2.astype(bf16)
    xs = _conv_mixer_mlp(xs, mx, norm_w[0], w_in, conv_w[0], w_out, w1, w2, layer=0, row_len=GRID_W,
                         tokens_per_mod=seq)
    cs = _conv_mixer_mlp(cs, mc, norm_w[0], w_in, conv_w[0], w_out, w1, w2, layer=0, row_len=t_ctx,
                         tokens_per_mod=batch * t_ctx)

    mx, mc = mod[1, :batch], mod[1, batch:batch + 1]
    nh = ML_HEADS
    nqk = nh * ML_DQK
    wq, wk, wv, wo = (ml_w_qkvo[0][:, lo:hi] for lo, hi in
                      ((0, nqk), (nqk, 2 * nqk), (2 * nqk, 2 * nqk + d), (2 * nqk + d, 2 * nqk + 2 * d)))
    w_proj_t = jnp.concatenate([wq, wv], axis=1).T.astype(bf16)
    w_proj = jnp.concatenate([wk, wo], axis=1).astype(bf16)
    zpad = jnp.zeros((d, GATE_LANES - 2 * nh), f32)
    w_gate = jnp.concatenate([ml_w_if[0, 0, :, :nh], ml_w_if[0, 1, :, :nh], zpad,
                              ml_w_if[0, 0, :, nh:], ml_w_if[0, 1, :, nh:], zpad], axis=1).astype(bf16)
    bpad = jnp.zeros((GATE_LANES - 2 * nh,), f32)
    b_gate = jnp.concatenate([ml_b_if[0, 0, :nh], ml_b_if[0, 1, :nh], bpad,
                              ml_b_if[0, 0, nh:], ml_b_if[0, 1, nh:], bpad])[None, :]
    qtx, kx, vtx, ogx, acx, grx = _ml_proj(xs, mx, norm_w[1], w_proj_t, w_proj, w_gate, b_gate,
                                           tokens_per_mod=seq)
    _, kc, vtc, _, _, grc = _ml_proj(cs, mc, norm_w[1], w_proj_t, w_proj, w_gate, b_gate,
                                     tokens_per_mod=batch * t_ctx)
    hf_t = _ml_scan((kc, vtc, grc), (qtx, kx, vtx, acx, grx), batch=batch, reverse=False)
    hb_t = _ml_scan((kc, vtc, grc), (qtx, kx, vtx, acx, grx), batch=batch, reverse=True)
    xs = _ml_readout_mlp(hf_t, hb_t, ogx, xs, mx, norm_w[1], ml_norm_w[0][None, :], ml_w_out[0].astype(bf16),
                         w1, w2, layer=1, tokens_per_mod=seq)
    return xs.reshape(batch, seq, d)
```

```python
import functools

import jax
import jax.numpy as jnp
from jax import lax
from jax.experimental import pallas as pl
from jax.experimental.pallas import tpu as pltpu

EPS = 1e-6
GRID_W = 64
ML_HEADS = 8
ML_DV = 128
ML_DQK = 64
ML_CHUNK = 128
GATE_LANES = 128
N_GATE_ROWS = 5
STATE_ROWS = ML_DV + 16
NEG_BIG = -1e30

VMEM_LIMIT_BYTES = 56 * 1024 * 1024
TOKEN_TILE = 512
ROW_PARTS = 2
SCAN_BLOCK = 1024
FF_CHUNK = 1024
CONV_COL_CHUNK = 512
ADALN_COL_TILE = 1536

f32 = jnp.float32
bf16 = jnp.bfloat16


def _dot(a, b):
    return jnp.dot(a, b, preferred_element_type=f32)


def _rms(x, w):
    return x * lax.rsqrt(jnp.mean(x * x, axis=-1, keepdims=True) + EPS) * w


def _resident(shape):
    nd = len(shape)
    return pl.BlockSpec(shape, lambda *_: (0,) * nd, pipeline_mode=pl.Buffered(1))


def _resident_layer(stacked_shape, layer):
    nd = len(stacked_shape) - 1
    return pl.BlockSpec((None,) + tuple(stacked_shape[1:]), lambda *_: (layer,) + (0,) * nd,
                        pipeline_mode=pl.Buffered(1))


def _params(*sem):
    return pltpu.CompilerParams(dimension_semantics=sem, vmem_limit_bytes=VMEM_LIMIT_BYTES)


def _adaln_kernel(c_ref, w_ref, b_ref, o_ref):
    c = c_ref[...]
    s = c * jax.nn.sigmoid(c)
    w = w_ref[0]
    sh = s.astype(bf16)
    sl = (s - sh.astype(f32)).astype(bf16)
    wh = w.astype(bf16)
    wl = (w - wh.astype(f32)).astype(bf16)
    o_ref[0] = _dot(sh, wh) + _dot(sh, wl) + _dot(sl, wh) + b_ref[0]


def _adaln(cc, mod_w, mod_b):
    depth, d, n = mod_w.shape
    rows = cc.shape[0]
    tn = ADALN_COL_TILE
    return pl.pallas_call(
        _adaln_kernel,
        grid=(depth, n // tn),
        in_specs=[pl.BlockSpec((rows, d), lambda i, j: (0, 0)),
                  pl.BlockSpec((1, d, tn), lambda i, j: (i, 0, j)),
                  pl.BlockSpec((1, 1, tn), lambda i, j: (i, 0, j))],
        out_specs=pl.BlockSpec((1, rows, tn), lambda i, j: (i, 0, j)),
        out_shape=jax.ShapeDtypeStruct((depth, rows, n), f32),
        compiler_params=_params("arbitrary", "arbitrary"),
        name="adaln",
    )(cc, mod_w, mod_b.reshape(depth, 1, n))


def _conv_mixer_kernel(x_ref, mod_ref, nw_ref, win_ref, cw_ref, wout_ref, w1_ref, w2_ref,
                       o_ref, z_ref, h_ref, x1_ref, acc_ref, *, row_len):
    mod = mod_ref[0]
    nw = nw_ref[...]
    tm, d = x_ref.shape
    cw = cw_ref[...]
    cc = CONV_COL_CHUNK
    parts = _row_parts(tm, row_len)

    for r in parts:
        h_ref[r, :] = (_rms(x_ref[r, :], nw[0:1]) * (1.0 + mod[1:2]) + mod[0:1]).astype(bf16)
    for r in parts:
        h = h_ref[r, :]
        pm = r.stop - r.start
        pos = lax.broadcasted_iota(jnp.int32, (pm, cc), 0) % row_len
        for j in range(d // cc):
            cs = slice(j * cc, (j + 1) * cc)
            b_gate = _dot(h, win_ref[:, j * cc:(j + 1) * cc])
            c_gate = _dot(h, win_ref[:, d + j * cc:d + (j + 1) * cc])
            u = c_gate * _dot(h, win_ref[:, 2 * d + j * cc:2 * d + (j + 1) * cc])
            u_prev = jnp.where(pos == 0, 0.0, pltpu.roll(u, 1, 0))
            u_next = jnp.where(pos == row_len - 1, 0.0, pltpu.roll(u, pm - 1, 0))
            y = cw[0:1, cs] * u_prev + cw[1:2, cs] * u + cw[2:3, cs] * u_next
            z_ref[r, cs] = (b_gate * y).astype(bf16)
    for r in parts:
        _mixer_residual(x_ref[r, :], _dot(z_ref[r, :], wout_ref[...]), mod, nw, x1_ref, h_ref, r)
    for r in parts:
        _mlp_matmuls(h_ref, acc_ref, w1_ref, w2_ref, r)
    for r in parts:
        _mlp_out(x1_ref, acc_ref, o_ref, mod, nw, r)


def _conv_mixer_mlp(x2, mod, nw, w_in, conv_w, w_out, w1, w2, *, layer, row_len, tokens_per_mod):
    n, d = x2.shape
    tm = TOKEN_TILE
    assert n % tm == 0 and tm % row_len == 0 and tokens_per_mod % tm == 0
    return pl.pallas_call(
        functools.partial(_conv_mixer_kernel, row_len=row_len),
        grid=(n // tm,),
        in_specs=[pl.BlockSpec((tm, d), lambda i: (i, 0)),
                  pl.BlockSpec((1, 6, d), lambda i: (i * tm // tokens_per_mod, 0, 0)),
                  _resident(nw.shape), _resident(w_in.shape), _resident(conv_w.shape),
                  _resident(w_out.shape), _resident_layer(w1.shape, layer), _resident_layer(w2.shape, layer)],
        out_specs=pl.BlockSpec((tm, d), lambda i: (i, 0)),
        out_shape=jax.ShapeDtypeStruct((n, d), f32),
        scratch_shapes=[pltpu.VMEM((tm, d), bf16), pltpu.VMEM((tm, d), bf16),
                        pltpu.VMEM((tm, d), f32), pltpu.VMEM((tm, d), f32)],
        compiler_params=_params("parallel"),
        name="conv_mixer_mlp",
    )(x2, mod, nw, w_in, conv_w, w_out, w1, w2)


def _row_parts(tm, multiple):
    parts = ROW_PARTS
    while (tm // parts) % multiple:
        parts //= 2
    pm = tm // parts
    assert parts >= 1 and pm * parts == tm and pm % 16 == 0
    return [slice(i * pm, (i + 1) * pm) for i in range(parts)]


def _mixer_residual(x, y_mix, mod, nw, x1_ref, h_ref, r):
    x1 = x + mod[2:3] * _rms(y_mix, nw[1:2])
    x1_ref[r, :] = x1
    h_ref[r, :] = (_rms(x1, nw[2:3]) * (1.0 + mod[4:5]) + mod[3:4]).astype(bf16)


def _mlp_matmuls(h_ref, acc_ref, w1_ref, w2_ref, r):
    ff = w1_ref.shape[1]
    h = h_ref[r, :]
    for j in range(ff // FF_CHUNK):
        a = jnp.maximum(_dot(h, w1_ref[:, j * FF_CHUNK:(j + 1) * FF_CHUNK]), 0.0)
        part = _dot((a * a).astype(bf16), w2_ref[j * FF_CHUNK:(j + 1) * FF_CHUNK, :])
        if j == 0:
            acc_ref[r, :] = part
        else:
            acc_ref[r, :] += part


def _mlp_out(x1_ref, acc_ref, o_ref, mod, nw, r):
    o_ref[r, :] = x1_ref[r, :] + mod[5:6] * _rms(acc_ref[r, :], nw[3:4])


def _log_sigmoid(x):
    return jnp.minimum(x, 0.0) - jnp.log1p(jnp.exp(-jnp.abs(x)))


def _ml_proj_kernel(x_ref, mod_ref, nw_ref, wpt_ref, wp_ref, wg_ref, bg_ref,
                    qt_ref, k_ref, vt_ref, o_ref, gr_ref):
    mod = mod_ref[0]
    nw = nw_ref[...]
    nqk = ML_HEADS * ML_DQK
    h32 = _rms(x_ref[...], nw[0:1]) * (1.0 + mod[1:2]) + mod[0:1]
    h = h32.astype(bf16)
    _gate_terms(_dot(h, wg_ref[...]) + bg_ref[...], gr_ref)
    k_ref[...] = _dot(h, wp_ref[:, 0:nqk]).astype(bf16)
    o_ref[...] = _dot(h, wp_ref[:, nqk:]).astype(bf16)
    h_t = h32.T.astype(bf16)
    qt_ref[...] = (_dot(wpt_ref[0:nqk, :], h_t) * (ML_DQK ** -0.5)).astype(bf16)
    vt_ref[...] = _dot(wpt_ref[nqk:, :], h_t).astype(bf16)


def _gate_terms(g, gr_ref):
    ng = 2 * ML_HEADS
    n = g.shape[0]
    ig = g[:, :GATE_LANES].T[0:ng]
    lf = _log_sigmoid(g[:, GATE_LANES:].T[0:ng])
    pos = lax.broadcasted_iota(jnp.int32, (ng, n), 1) % ML_CHUNK
    is_fwd = lax.broadcasted_iota(jnp.int32, (ng, n), 0) < ML_HEADS

    def chunk_scan(v, op, fill):
        pre, suf = v, v
        step = 1
        while step < ML_CHUNK:
            pre = op(pre, jnp.where(pos >= step, pltpu.roll(pre, step, 1), fill))
            suf = op(suf, jnp.where(pos < ML_CHUNK - step, pltpu.roll(suf, n - step, 1), fill))
            step *= 2
        return pre, suf

    b_pre, b_suf = chunk_scan(lf, jnp.add, 0.0)
    b = jnp.where(is_fwd, b_pre, b_suf)
    tot = b_pre + b_suf - lf
    a = ig - b
    a_pre, a_suf = chunk_scan(a, jnp.maximum, NEG_BIG)
    cm = jnp.where(is_fwd, a_pre, a_suf)
    am = jnp.maximum(a_pre, a_suf)
    for i, arr in enumerate((a, cm, am, b, tot)):
        for ch in range(n // ML_CHUNK):
            gr_ref[ch, i] = arr[:, ch * ML_CHUNK:(ch + 1) * ML_CHUNK]


def _ml_proj(x2, mod, nw, w_proj_t, w_proj, w_gate, b_gate, *, tokens_per_mod):
    n, d = x2.shape
    tm = TOKEN_TILE
    assert n % tm == 0 and tokens_per_mod % tm == 0 and tm % ML_CHUNK == 0
    nqk = ML_HEADS * ML_DQK
    nv = ML_HEADS * ML_DV
    tok = lambda width: pl.BlockSpec((tm, width), lambda i: (i, 0))
    tok_t = lambda width: pl.BlockSpec((width, tm), lambda i: (0, i))
    gr_shape = (n // ML_CHUNK, N_GATE_ROWS, 2 * ML_HEADS, ML_CHUNK)
    return pl.pallas_call(
        _ml_proj_kernel,
        grid=(n // tm,),
        in_specs=[tok(d),
                  pl.BlockSpec((1, 6, d), lambda i: (i * tm // tokens_per_mod, 0, 0)),
                  _resident(nw.shape), _resident(w_proj_t.shape), _resident(w_proj.shape),
                  _resident(w_gate.shape), _resident(b_gate.shape)],
        out_specs=[tok_t(nqk), tok(nqk), tok_t(nv), tok(nv),
                   pl.BlockSpec((tm // ML_CHUNK,) + gr_shape[1:], lambda i: (i, 0, 0, 0))],
        out_shape=[jax.ShapeDtypeStruct((nqk, n), bf16), jax.ShapeDtypeStruct((n, nqk), bf16),
                   jax.ShapeDtypeStruct((nv, n), bf16), jax.ShapeDtypeStruct((n, nv), bf16),
                   jax.ShapeDtypeStruct(gr_shape, f32)],
        compiler_params=_params("parallel"),
        name="mlstm_proj",
    )(x2, mod, nw, w_proj_t, w_proj, w_gate, b_gate)


def _ml_block(qt_ref, k_ref, vt_ref, gr_ref, o_ref, s_ref, m_ref, rhs_ref, reverse):
    L = ML_CHUNK
    emit = o_ref is not None
    nch = k_ref.shape[0] // L
    order = list(range(nch - 1, -1, -1) if reverse else range(nch))
    if emit:
        s_idx = lax.broadcasted_iota(jnp.int32, (L, L), 0)
        t_idx = lax.broadcasted_iota(jnp.int32, (L, L), 1)
        mask = (s_idx >= t_idx) if reverse else (s_idx <= t_idx)
        zero_half = jnp.zeros((ML_DQK, L), bf16)

        def pad_half(q_half, e):
            return jnp.concatenate([q_half, zero_half] if e == 0 else [zero_half, q_half], axis=0)

    first_row = lax.broadcasted_iota(jnp.int32, (STATE_ROWS - ML_DV, L), 0) == 0
    ones_row = jnp.where(first_row, 1.0, 0.0).astype(bf16)
    g0 = ML_HEADS if reverse else 0

    def gate(ch, i, hd):
        return gr_ref[ch, i, g0 + hd:g0 + hd + 1, :]

    def k_pair(ch, pair):
        return k_ref[ch * L:(ch + 1) * L, pair * 2 * ML_DQK:(pair + 1) * 2 * ML_DQK]

    for i, ch in enumerate(order):
        for hd in range(ML_HEADS):
            m_ref[i + 1, hd] = gate(ch, 4, hd) + jnp.maximum(m_ref[i, hd], gate(ch, 2, hd))

    def stage1(i):
        ch = order[i]
        rows = slice(ch * L, (ch + 1) * L)
        a_rows = gr_ref[ch, 0, g0:g0 + ML_HEADS, :]
        a_cols = jnp.concatenate([a_rows, jnp.zeros((L - ML_HEADS, L), f32)], axis=0).T
        for pair in range(ML_HEADS // 2):
            qt = (qt_ref[(2 * pair) * ML_DQK:(2 * pair + 1) * ML_DQK, rows],
                  qt_ref[(2 * pair + 1) * ML_DQK:(2 * pair + 2) * ML_DQK, rows])
            kq = _dot(k_pair(ch, pair), jnp.concatenate([pad_half(qt[0], 0), pad_half(qt[1], 1)], axis=1))
            for e in range(2):
                hd = 2 * pair + e
                m_row = m_ref[i, hd]
                m_run = jnp.maximum(m_row, gate(ch, 1, hd))
                decay = jnp.exp(jnp.where(mask, a_cols[:, hd:hd + 1] - m_run, NEG_BIG))
                s_mat = kq[:, e * L:(e + 1) * L] * decay
                rhs_ref[i % 2, hd, 0:L, :] = s_mat.astype(bf16)
                q_scaled = (qt[e].astype(f32) * jnp.exp(m_row - m_run)).astype(bf16)
                rhs_ref[i % 2, hd, L:2 * L, :] = pad_half(q_scaled, e)

    def stage2(i):
        ch = order[i]
        rows = slice(ch * L, (ch + 1) * L)
        for hd in range(ML_HEADS):
            vt_h = vt_ref[hd * ML_DV:(hd + 1) * ML_DV, rows]
            m_row = m_ref[i, hd]
            state = s_ref[hd]
            if emit:
                lhs = jnp.concatenate([jnp.concatenate([vt_h, ones_row], axis=0), state.astype(bf16)], axis=1)
                res = _dot(lhs, rhs_ref[i % 2, hd])
                floor = jnp.exp(-(gate(ch, 3, hd) + jnp.maximum(m_row, gate(ch, 1, hd))))
                inv = 1.0 / jnp.maximum(jnp.abs(res[ML_DV:ML_DV + 1]), floor)
                o_ref[hd * ML_DV:(hd + 1) * ML_DV, rows] = (res[:ML_DV] * inv).astype(o_ref.dtype)
            m_last = jnp.maximum(m_row, gate(ch, 2, hd))
            wk = jnp.exp(gate(ch, 0, hd) - m_last)
            lhs = jnp.concatenate([vt_h * wk.astype(bf16), jnp.where(first_row, wk, 0.0).astype(bf16)], axis=0)
            kv = _dot(lhs, k_pair(ch, hd // 2))
            s_ref[hd] = jnp.exp(m_row - m_last) * state + kv

    if emit:
        stage1(0)
    for i in range(nch):
        if emit and i + 1 < nch:
            stage1(i + 1)
        stage2(i)
    m_ref[0] = m_ref[nch]


def _ml_scan_kernel(kc_ref, vtc_ref, grc_ref, qtx_ref, kx_ref, vtx_ref, grx_ref,
                    o_ref, s_ref, m_ref, rhs_ref, *, reverse):
    j = pl.program_id(1)

    @pl.when(j == 0)
    def _():
        s_ref[...] = jnp.zeros_like(s_ref)
        m_ref[0] = jnp.zeros(m_ref.shape[1:], f32)
        _ml_block(None, kc_ref, vtc_ref, grc_ref, None, s_ref, m_ref, rhs_ref, reverse)

    _ml_block(qtx_ref, kx_ref, vtx_ref, grx_ref, o_ref, s_ref, m_ref, rhs_ref, reverse)


def _ml_scan(ctx_parts, lat_parts, *, batch, reverse):
    kc, vtc, grc = ctx_parts
    qtx, kx, vtx, grx = lat_parts
    t_ctx = kc.shape[0] // batch
    t_lat = kx.shape[0] // batch
    tb = SCAN_BLOCK
    assert t_lat % tb == 0 and tb % ML_CHUNK == 0 and t_ctx % ML_CHUNK == 0
    nblk = t_lat // tb
    nqk = ML_HEADS * ML_DQK
    nv = ML_HEADS * ML_DV
    gr_tail = (N_GATE_ROWS, 2 * ML_HEADS, ML_CHUNK)

    def lat_blk(b, j):
        return b * nblk + (nblk - 1 - j if reverse else j)

    lat_tok = lambda width: pl.BlockSpec((tb, width), lambda b, j: (lat_blk(b, j), 0))
    lat_tok_t = lambda width: pl.BlockSpec((width, tb), lambda b, j: (0, lat_blk(b, j)))
    return pl.pallas_call(
        functools.partial(_ml_scan_kernel, reverse=reverse),
        grid=(batch, nblk),
        in_specs=[pl.BlockSpec((t_ctx, nqk), lambda b, j: (b, 0)),
                  pl.BlockSpec((nv, t_ctx), lambda b, j: (0, b)),
                  pl.BlockSpec((t_ctx // ML_CHUNK,) + gr_tail, lambda b, j: (b, 0, 0, 0)),
                  lat_tok_t(nqk), lat_tok(nqk), lat_tok_t(nv),
                  pl.BlockSpec((tb // ML_CHUNK,) + gr_tail, lambda b, j: (lat_blk(b, j), 0, 0, 0))],
        out_specs=lat_tok_t(nv),
        out_shape=jax.ShapeDtypeStruct((nv, batch * t_lat), bf16),
        scratch_shapes=[pltpu.VMEM((ML_HEADS, STATE_ROWS, 2 * ML_DQK), f32),
                        pltpu.VMEM((max(tb, t_ctx) // ML_CHUNK + 1, ML_HEADS, 1, ML_CHUNK), f32),
                        pltpu.VMEM((2, ML_HEADS, 2 * ML_CHUNK, ML_CHUNK), bf16)],
        compiler_params=_params("parallel", "arbitrary"),
        name="mlstm_scan_bwd" if reverse else "mlstm_scan_fwd",
    )(kc, vtc, grc, qtx, kx, vtx, grx)


def _ml_readout_kernel(hf_ref, hb_ref, og_ref, x_ref, mod_ref, nw_ref, mnw_ref, wout_ref, w1_ref, w2_ref,
                       o_ref, z_ref, h_ref, x1_ref, acc_ref):
    mod = mod_ref[0]
    nw = nw_ref[...]
    mnw = mnw_ref[...]
    parts = _row_parts(x_ref.shape[0], 128)
    for r in parts:
        for hd in range(ML_HEADS):
            vs = slice(hd * ML_DV, (hd + 1) * ML_DV)
            hs = hf_ref[vs, r].astype(f32) + hb_ref[vs, r].astype(f32)
            hn = hs * lax.rsqrt(jnp.mean(hs * hs, axis=0, keepdims=True) + EPS)
            z_ref[r, vs] = (jax.nn.sigmoid(og_ref[r, vs].astype(f32)) * (hn.T * mnw[:, vs])).astype(bf16)
    for r in parts:
        _mixer_residual(x_ref[r, :], _dot(z_ref[r, :], wout_ref[...]), mod, nw, x1_ref, h_ref, r)
    for r in parts:
        _mlp_matmuls(h_ref, acc_ref, w1_ref, w2_ref, r)
    for r in parts:
        _mlp_out(x1_ref, acc_ref, o_ref, mod, nw, r)


def _ml_readout_mlp(hf_t, hb_t, og, x2, mod, nw, ml_norm_w, w_out, w1, w2, *, layer, tokens_per_mod):
    n, d = x2.shape
    tm = TOKEN_TILE
    nv = ML_HEADS * ML_DV
    assert n % tm == 0 and tokens_per_mod % tm == 0
    tok = lambda width: pl.BlockSpec((tm, width), lambda i: (i, 0))
    tok_t = lambda width: pl.BlockSpec((width, tm), lambda i: (0, i))
    return pl.pallas_call(
        _ml_readout_kernel,
        grid=(n // tm,),
        in_specs=[tok_t(nv), tok_t(nv), tok(nv), tok(d),
                  pl.BlockSpec((1, 6, d), lambda i: (i * tm // tokens_per_mod, 0, 0)),
                  _resident(nw.shape), _resident(ml_norm_w.shape), _resident(w_out.shape),
                  _resident_layer(w1.shape, layer), _resident_layer(w2.shape, layer)],
        out_specs=tok(d),
        out_shape=jax.ShapeDtypeStruct((n, d), f32),
        scratch_shapes=[pltpu.VMEM((tm, nv), bf16), pltpu.VMEM((tm, d), bf16),
                        pltpu.VMEM((tm, d), f32), pltpu.VMEM((tm, d), f32)],
        compiler_params=_params("parallel"),
        name="mlstm_readout_mlp",
    )(hf_t, hb_t, og, x2, mod, nw, ml_norm_w, w_out, w1, w2)


def kernel(x, c, ctx, c_ctx, norm_w, mod_w, mod_b, mlp_w1, mlp_w2, conv_w_in, conv_w, conv_w_out,
           ml_w_qkvo, ml_w_if, ml_b_if, ml_norm_w, ml_w_out):
    batch, seq, d = x.shape
    t_ctx = ctx.shape[1]
    depth = norm_w.shape[0]
    assert depth == 2 and seq % GRID_W == 0 and d == ML_HEADS * ML_DV

    pad_rows = -(batch + 1) % 8
    cc = jnp.concatenate([c, c_ctx[None, :], jnp.zeros((pad_rows, d), f32)], axis=0)
    mod = _adaln(cc, mod_w, mod_b).reshape(depth, batch + 1 + pad_rows, 6, d)

    xs = x.reshape(batch * seq, d)
    cs = ctx.reshape(batch * t_ctx, d)

    mx, mc = mod[0, :batch], mod[0, batch:batch + 1]
    w_in = conv_w_in[0].astype(bf16)
    w_out = conv_w_out[0].astype(bf16)
    w1 = mlp_w1.astype(bf16)
    w2 = mlp_w2.astype(bf16)
    xs = _conv_mixer_mlp(xs, mx, norm_w[0], w_in, conv_w[0], w_out, w1, w2, layer=0, row_len=GRID_W,
                         tokens_per_mod=seq)
    cs = _conv_mixer_mlp(cs, mc, norm_w[0], w_in, conv_w[0], w_out, w1, w2, layer=0, row_len=t_ctx,
                         tokens_per_mod=batch * t_ctx)

    mx, mc = mod[1, :batch], mod[1, batch:batch + 1]
    nh = ML_HEADS
    nqk = nh * ML_DQK
    wq, wk, wv, wo = (ml_w_qkvo[0][:, lo:hi] for lo, hi in
                      ((0, nqk), (nqk, 2 * nqk), (2 * nqk, 2 * nqk + d), (2 * nqk + d, 2 * nqk + 2 * d)))
    w_proj_t = jnp.concatenate([wq, wv], axis=1).T.astype(bf16)
    w_proj = jnp.concatenate([wk, wo], axis=1).astype(bf16)
    zpad = jnp.zeros((d, GATE_LANES - 2 * nh), f32)
    w_gate = jnp.concatenate([ml_w_if[0, 0, :, :nh], ml_w_if[0, 1, :, :nh], zpad,
                              ml_w_if[0, 0, :, nh:], ml_w_if[0, 1, :, nh:], zpad], axis=1).astype(bf16)
    bpad = jnp.zeros((GATE_LANES - 2 * nh,), f32)
    b_gate = jnp.concatenate([ml_b_if[0, 0, :nh], ml_b_if[0, 1, :nh], bpad,
                              ml_b_if[0, 0, nh:], ml_b_if[0, 1, nh:], bpad])[None, :]
    qtx, kx, vtx, ogx, grx = _ml_proj(xs, mx, norm_w[1], w_proj_t, w_proj, w_gate, b_gate, tokens_per_mod=seq)
    _, kc, vtc, _, grc = _ml_proj(cs, mc, norm_w[1], w_proj_t, w_proj, w_gate, b_gate,
                                  tokens_per_mod=batch * t_ctx)
    hf_t = _ml_scan((kc, vtc, grc), (qtx, kx, vtx, grx), batch=batch, reverse=False)
    hb_t = _ml_scan((kc, vtc, grc), (qtx, kx, vtx, grx), batch=batch, reverse=True)
    xs = _ml_readout_mlp(hf_t, hb_t, ogx, xs, mx, norm_w[1], ml_norm_w[0][None, :], ml_w_out[0].astype(bf16),
                         w1, w2, layer=1, tokens_per_mod=seq)
    return xs.reshape(batch, seq, d)
```

```python
import functools

import jax
import jax.numpy as jnp
from jax import lax
from jax.experimental import pallas as pl
from jax.experimental.pallas import tpu as pltpu

EPS = 1e-6
GRID_W = 64
ML_HEADS = 8
ML_DV = 128
ML_DQK = 64
ML_CHUNK = 128
GATE_LANES = 128
N_GATE_ROWS = 5
STATE_ROWS = ML_DV + 16
NEG_BIG = -1e30

VMEM_LIMIT_BYTES = 56 * 1024 * 1024
TOKEN_TILE = 512
ROW_PARTS = 2
SCAN_BLOCK = 1024
FF_CHUNK = 1024
CONV_COL_CHUNK = 512
ADALN_COL_TILE = 1536

f32 = jnp.float32
bf16 = jnp.bfloat16


def _dot(a, b):
    return jnp.dot(a, b, preferred_element_type=f32)


def _rms(x, w):
    return x * lax.rsqrt(jnp.mean(x * x, axis=-1, keepdims=True) + EPS) * w


def _resident(shape):
    nd = len(shape)
    return pl.BlockSpec(shape, lambda *_: (0,) * nd, pipeline_mode=pl.Buffered(1))


def _resident_layer(stacked_shape, layer):
    nd = len(stacked_shape) - 1
    return pl.BlockSpec((None,) + tuple(stacked_shape[1:]), lambda *_: (layer,) + (0,) * nd,
                        pipeline_mode=pl.Buffered(1))


def _params(*sem):
    return pltpu.CompilerParams(dimension_semantics=sem, vmem_limit_bytes=VMEM_LIMIT_BYTES)


def _adaln_kernel(c_ref, w_ref, b_ref, o_ref):
    c = c_ref[...]
    s = c * jax.nn.sigmoid(c)
    w = w_ref[0]
    sh = s.astype(bf16)
    sl = (s - sh.astype(f32)).astype(bf16)
    wh = w.astype(bf16)
    wl = (w - wh.astype(f32)).astype(bf16)
    o_ref[0] = _dot(sh, wh) + _dot(sh, wl) + _dot(sl, wh) + b_ref[0]


def _adaln(cc, mod_w, mod_b):
    depth, d, n = mod_w.shape
    rows = cc.shape[0]
    tn = ADALN_COL_TILE
    return pl.pallas_call(
        _adaln_kernel,
        grid=(depth, n // tn),
        in_specs=[pl.BlockSpec((rows, d), lambda i, j: (0, 0)),
                  pl.BlockSpec((1, d, tn), lambda i, j: (i, 0, j)),
                  pl.BlockSpec((1, 1, tn), lambda i, j: (i, 0, j))],
        out_specs=pl.BlockSpec((1, rows, tn), lambda i, j: (i, 0, j)),
        out_shape=jax.ShapeDtypeStruct((depth, rows, n), f32),
        compiler_params=_params("arbitrary", "arbitrary"),
        name="adaln",
    )(cc, mod_w, mod_b.reshape(depth, 1, n))


def _conv_mixer_kernel(x_ref, mod_ref, nw_ref, win_ref, cw_ref, wout_ref, w1_ref, w2_ref,
                       o_ref, z_ref, h_ref, x1_ref, acc_ref, *, row_len):
    mod = mod_ref[0]
    nw = nw_ref[...]
    tm, d = x_ref.shape
    cw = cw_ref[...]
    cc = CONV_COL_CHUNK
    parts = _row_parts(tm, row_len)

    for r in parts:
        h_ref[r, :] = (_rms(x_ref[r, :], nw[0:1]) * (1.0 + mod[1:2]) + mod[0:1]).astype(bf16)
    for r in parts:
        h = h_ref[r, :]
        pm = r.stop - r.start
        pos = lax.broadcasted_iota(jnp.int32, (pm, cc), 0) % row_len
        for j in range(d // cc):
            cs = slice(j * cc, (j + 1) * cc)
            b_gate = _dot(h, win_ref[:, j * cc:(j + 1) * cc])
            c_gate = _dot(h, win_ref[:, d + j * cc:d + (j + 1) * cc])
            u = c_gate * _dot(h, win_ref[:, 2 * d + j * cc:2 * d + (j + 1) * cc])
            u_prev = jnp.where(pos == 0, 0.0, pltpu.roll(u, 1, 0))
            u_next = jnp.where(pos == row_len - 1, 0.0, pltpu.roll(u, pm - 1, 0))
            y = cw[0:1, cs] * u_prev + cw[1:2, cs] * u + cw[2:3, cs] * u_next
            z_ref[r, cs] = (b_gate * y).astype(bf16)
    for r in parts:
        _mixer_residual(x_ref[r, :], _dot(z_ref[r, :], wout_ref[...]), mod, nw, x1_ref, h_ref, r)
    _mlp_matmuls(h_ref, acc_ref, w1_ref, w2_ref, parts)
    for r in parts:
        _mlp_out(x1_ref, acc_ref, o_ref, mod, nw, r)


def _conv_mixer_mlp(x2, mod, nw, w_in, conv_w, w_out, w1, w2, *, layer, row_len, tokens_per_mod):
    n, d = x2.shape
    tm = TOKEN_TILE
    assert n % tm == 0 and tm % row_len == 0 and tokens_per_mod % tm == 0
    return pl.pallas_call(
        functools.partial(_conv_mixer_kernel, row_len=row_len),
        grid=(n // tm,),
        in_specs=[pl.BlockSpec((tm, d), lambda i: (i, 0)),
                  pl.BlockSpec((1, 6, d), lambda i: (i * tm // tokens_per_mod, 0, 0)),
                  _resident(nw.shape), _resident(w_in.shape), _resident(conv_w.shape),
                  _resident(w_out.shape), _resident_layer(w1.shape, layer), _resident_layer(w2.shape, layer)],
        out_specs=pl.BlockSpec((tm, d), lambda i: (i, 0)),
        out_shape=jax.ShapeDtypeStruct((n, d), f32),
        scratch_shapes=[pltpu.VMEM((tm, d), bf16), pltpu.VMEM((tm, d), bf16),
                        pltpu.VMEM((tm, d), f32), pltpu.VMEM((tm, d), f32)],
        compiler_params=_params("parallel"),
        name="conv_mixer_mlp",
    )(x2, mod, nw, w_in, conv_w, w_out, w1, w2)


def _row_parts(tm, multiple):
    parts = ROW_PARTS
    while (tm // parts) % multiple:
        parts //= 2
    pm = tm // parts
    assert parts >= 1 and pm * parts == tm and pm % 16 == 0
    return [slice(i * pm, (i + 1) * pm) for i in range(parts)]


def _mixer_residual(x, y_mix, mod, nw, x1_ref, h_ref, r):
    x1 = x + mod[2:3] * _rms(y_mix, nw[1:2])
    x1_ref[r, :] = x1
    h_ref[r, :] = (_rms(x1, nw[2:3]) * (1.0 + mod[4:5]) + mod[3:4]).astype(bf16)


def _mlp_matmuls(h_ref, acc_ref, w1_ref, w2_ref, parts):
    ff = w1_ref.shape[1]
    for j in range(ff // FF_CHUNK):
        for r in parts:
            a = jnp.maximum(_dot(h_ref[r, :], w1_ref[:, j * FF_CHUNK:(j + 1) * FF_CHUNK]), 0.0)
            part = _dot((a * a).astype(bf16), w2_ref[j * FF_CHUNK:(j + 1) * FF_CHUNK, :])
            if j == 0:
                acc_ref[r, :] = part
            else:
                acc_ref[r, :] += part


def _mlp_out(x1_ref, acc_ref, o_ref, mod, nw, r):
    o_ref[r, :] = x1_ref[r, :] + mod[5:6] * _rms(acc_ref[r, :], nw[3:4])


def _log_sigmoid(x):
    return jnp.minimum(x, 0.0) - jnp.log1p(jnp.exp(-jnp.abs(x)))


def _ml_proj_kernel(x_ref, mod_ref, nw_ref, wpt_ref, wp_ref, bg_ref,
                    qt_ref, k_ref, vt_ref, o_ref, ac_ref, gr_ref):
    mod = mod_ref[0]
    nw = nw_ref[...]
    nqk = ML_HEADS * ML_DQK
    ng = 2 * ML_HEADS
    h32 = _rms(x_ref[...], nw[0:1]) * (1.0 + mod[1:2]) + mod[0:1]
    h = h32.astype(bf16)
    h_t = h32.T.astype(bf16)
    qg_t = _dot(wpt_ref[0:nqk + 2 * ng, :], h_t)
    g_t = qg_t[nqk:, :] + bg_ref[...]
    _gate_terms(g_t[0:ng], g_t[ng:], ac_ref, gr_ref)
    qt_ref[...] = qg_t[0:nqk, :] * (ML_DQK ** -0.5)
    k_ref[...] = _dot(h, wp_ref[:, 0:nqk]).astype(bf16)
    o_ref[...] = _dot(h, wp_ref[:, nqk:]).astype(bf16)
    vt_ref[...] = _dot(wpt_ref[nqk + 2 * ng:, :], h_t).astype(bf16)


def _gate_terms(ig, fg, ac_ref, gr_ref):
    ng, n = ig.shape
    lf = _log_sigmoid(fg)
    pos = lax.broadcasted_iota(jnp.int32, (ng, n), 1) % ML_CHUNK
    is_fwd = lax.broadcasted_iota(jnp.int32, (ng, n), 0) < ML_HEADS

    def chunk_scan(v, op, fill):
        pre, suf = v, v
        step = 1
        while step < ML_CHUNK:
            pre = op(pre, jnp.where(pos >= step, pltpu.roll(pre, step, 1), fill))
            suf = op(suf, jnp.where(pos < ML_CHUNK - step, pltpu.roll(suf, n - step, 1), fill))
            step *= 2
        return pre, suf

    b_pre, b_suf = chunk_scan(lf, jnp.add, 0.0)
    b = jnp.where(is_fwd, b_pre, b_suf)
    tot = b_pre + b_suf - lf
    a = ig - b
    a_pre, a_suf = chunk_scan(a, jnp.maximum, NEG_BIG)
    cm = jnp.where(is_fwd, a_pre, a_suf)
    am = jnp.maximum(a_pre, a_suf)
    for i, arr in enumerate((a, cm, am, b, tot)):
        for ch in range(n // ML_CHUNK):
            gr_ref[ch, i] = arr[:, ch * ML_CHUNK:(ch + 1) * ML_CHUNK]
    ac_ref[...] = jnp.concatenate([a, jnp.zeros((GATE_LANES - ng, n), f32)], axis=0).T


def _ml_proj(x2, mod, nw, w_proj_t, w_proj, b_gate, *, tokens_per_mod):
    n, d = x2.shape
    tm = TOKEN_TILE
    assert n % tm == 0 and tokens_per_mod % tm == 0 and tm % ML_CHUNK == 0
    nqk = ML_HEADS * ML_DQK
    nv = ML_HEADS * ML_DV
    tok = lambda width: pl.BlockSpec((tm, width), lambda i: (i, 0))
    tok_t = lambda width: pl.BlockSpec((width, tm), lambda i: (0, i))
    gr_shape = (n // ML_CHUNK, N_GATE_ROWS, 2 * ML_HEADS, ML_CHUNK)
    return pl.pallas_call(
        _ml_proj_kernel,
        grid=(n // tm,),
        in_specs=[tok(d),
                  pl.BlockSpec((1, 6, d), lambda i: (i * tm // tokens_per_mod, 0, 0)),
                  _resident(nw.shape), _resident(w_proj_t.shape), _resident(w_proj.shape),
                  _resident(b_gate.shape)],
        out_specs=[tok_t(nqk), tok(nqk), tok_t(nv), tok(nv), tok(GATE_LANES),
                   pl.BlockSpec((tm // ML_CHUNK,) + gr_shape[1:], lambda i: (i, 0, 0, 0))],
        out_shape=[jax.ShapeDtypeStruct((nqk, n), f32), jax.ShapeDtypeStruct((n, nqk), bf16),
                   jax.ShapeDtypeStruct((nv, n), bf16), jax.ShapeDtypeStruct((n, nv), bf16),
                   jax.ShapeDtypeStruct((n, GATE_LANES), f32), jax.ShapeDtypeStruct(gr_shape, f32)],
        compiler_params=_params("parallel"),
        name="mlstm_proj",
    )(x2, mod, nw, w_proj_t, w_proj, b_gate)


def _ml_block(qt_ref, k_ref, vt_ref, ac_ref, gr_ref, o_ref, s_ref, m_ref, rhs_ref, reverse):
    L = ML_CHUNK
    emit = o_ref is not None
    nch = k_ref.shape[0] // L
    order = list(range(nch - 1, -1, -1) if reverse else range(nch))
    if emit:
        s_idx = lax.broadcasted_iota(jnp.int32, (L, L), 0)
        t_idx = lax.broadcasted_iota(jnp.int32, (L, L), 1)
        mask = (s_idx >= t_idx) if reverse else (s_idx <= t_idx)
        zero_half = jnp.zeros((ML_DQK, L), bf16)

        def pad_half(q_half, e):
            return jnp.concatenate([q_half, zero_half] if e == 0 else [zero_half, q_half], axis=0)

    first_row = lax.broadcasted_iota(jnp.int32, (STATE_ROWS - ML_DV, L), 0) == 0
    ones_row = jnp.where(first_row, 1.0, 0.0).astype(bf16)
    g0 = ML_HEADS if reverse else 0

    def gate(ch, i, hd):
        return gr_ref[ch, i, g0 + hd:g0 + hd + 1, :]

    def k_pair(ch, pair):
        return k_ref[ch * L:(ch + 1) * L, pair * 2 * ML_DQK:(pair + 1) * 2 * ML_DQK]

    for i, ch in enumerate(order):
        for hd in range(ML_HEADS):
            m_ref[i + 1, hd] = gate(ch, 4, hd) + jnp.maximum(m_ref[i, hd], gate(ch, 2, hd))

    def stage1(i):
        ch = order[i]
        rows = slice(ch * L, (ch + 1) * L)
        for pair in range(ML_HEADS // 2):
            qt = (qt_ref[(2 * pair) * ML_DQK:(2 * pair + 1) * ML_DQK, rows],
                  qt_ref[(2 * pair + 1) * ML_DQK:(2 * pair + 2) * ML_DQK, rows])
            kq = _dot(k_pair(ch, pair), jnp.concatenate([pad_half(qt[0].astype(bf16), 0),
                                                         pad_half(qt[1].astype(bf16), 1)], axis=1))
            for e in range(2):
                hd = 2 * pair + e
                m_row = m_ref[i, hd]
                m_run = jnp.maximum(m_row, gate(ch, 1, hd))
                a_col = ac_ref[rows, g0 + hd:g0 + hd + 1]
                s_mat = kq[:, e * L:(e + 1) * L] * jnp.exp(jnp.where(mask, a_col - m_run, NEG_BIG))
                rhs_ref[i % 2, hd, 0:L, :] = s_mat.astype(bf16)
                rhs_ref[i % 2, hd, L:2 * L, :] = pad_half((qt[e] * jnp.exp(m_row - m_run)).astype(bf16), e)

    def stage2(i):
        ch = order[i]
        rows = slice(ch * L, (ch + 1) * L)
        for pair in range(ML_HEADS // 2):
            kv_lhs, scale = [], []
            for hd in (2 * pair, 2 * pair + 1):
                vt_h = vt_ref[hd * ML_DV:(hd + 1) * ML_DV, rows]
                m_row = m_ref[i, hd]
                if emit:
                    lhs = jnp.concatenate([jnp.concatenate([vt_h, ones_row], axis=0),
                                           s_ref[hd].astype(bf16)], axis=1)
                    res = _dot(lhs, rhs_ref[i % 2, hd])
                    floor = jnp.exp(-(gate(ch, 3, hd) + jnp.maximum(m_row, gate(ch, 1, hd))))
                    inv = 1.0 / jnp.maximum(jnp.abs(res[ML_DV:ML_DV + 1]), floor)
                    o_ref[hd * ML_DV:(hd + 1) * ML_DV, rows] = (res[:ML_DV] * inv).astype(o_ref.dtype)
                m_last = jnp.maximum(m_row, gate(ch, 2, hd))
                wk = jnp.exp(gate(ch, 0, hd) - m_last)
                kv_lhs += [vt_h * wk.astype(bf16), jnp.where(first_row, wk, 0.0).astype(bf16)]
                scale.append(jnp.exp(m_row - m_last))
            kv = _dot(jnp.concatenate(kv_lhs, axis=0), k_pair(ch, pair))
            for e in range(2):
                hd = 2 * pair + e
                s_ref[hd] = scale[e] * s_ref[hd] + kv[e * STATE_ROWS:(e + 1) * STATE_ROWS]

    if emit:
        stage1(0)
    for i in range(nch):
        if emit and i + 1 < nch:
            stage1(i + 1)
        stage2(i)
    m_ref[0] = m_ref[nch]


def _ml_scan_kernel(kc_ref, vtc_ref, grc_ref, qtx_ref, kx_ref, vtx_ref, acx_ref, grx_ref,
                    o_ref, s_ref, m_ref, rhs_ref, *, reverse):
    j = pl.program_id(1)

    @pl.when(j == 0)
    def _():
        s_ref[...] = jnp.zeros_like(s_ref)
        m_ref[0] = jnp.zeros(m_ref.shape[1:], f32)
        _ml_block(None, kc_ref, vtc_ref, None, grc_ref, None, s_ref, m_ref, rhs_ref, reverse)

    _ml_block(qtx_ref, kx_ref, vtx_ref, acx_ref, grx_ref, o_ref, s_ref, m_ref, rhs_ref, reverse)


def _ml_scan(ctx_parts, lat_parts, *, batch, reverse):
    kc, vtc, grc = ctx_parts
    qtx, kx, vtx, acx, grx = lat_parts
    t_ctx = kc.shape[0] // batch
    t_lat = kx.shape[0] // batch
    tb = SCAN_BLOCK
    assert t_lat % tb == 0 and tb % ML_CHUNK == 0 and t_ctx % ML_CHUNK == 0
    nblk = t_lat // tb
    nqk = ML_HEADS * ML_DQK
    nv = ML_HEADS * ML_DV
    gr_tail = (N_GATE_ROWS, 2 * ML_HEADS, ML_CHUNK)

    def lat_blk(b, j):
        return b * nblk + (nblk - 1 - j if reverse else j)

    lat_tok = lambda width: pl.BlockSpec((tb, width), lambda b, j: (lat_blk(b, j), 0))
    lat_tok_t = lambda width: pl.BlockSpec((width, tb), lambda b, j: (0, lat_blk(b, j)))
    return pl.pallas_call(
        functools.partial(_ml_scan_kernel, reverse=reverse),
        grid=(batch, nblk),
        in_specs=[pl.BlockSpec((t_ctx, nqk), lambda b, j: (b, 0)),
                  pl.BlockSpec((nv, t_ctx), lambda b, j: (0, b)),
                  pl.BlockSpec((t_ctx // ML_CHUNK,) + gr_tail, lambda b, j: (b, 0, 0, 0)),
                  lat_tok_t(nqk), lat_tok(nqk), lat_tok_t(nv), lat_tok(GATE_LANES),
                  pl.BlockSpec((tb // ML_CHUNK,) + gr_tail, lambda b, j: (lat_blk(b, j), 0, 0, 0))],
        out_specs=lat_tok_t(nv),
        out_shape=jax.ShapeDtypeStruct((nv, batch * t_lat), bf16),
        scratch_shapes=[pltpu.VMEM((ML_HEADS, STATE_ROWS, 2 * ML_DQK), f32),
                        pltpu.VMEM((max(tb, t_ctx) // ML_CHUNK + 1, ML_HEADS, 1, ML_CHUNK), f32),
                        pltpu.VMEM((2, ML_HEADS, 2 * ML_CHUNK, ML_CHUNK), bf16)],
        compiler_params=_params("parallel", "arbitrary"),
        name="mlstm_scan_bwd" if reverse else "mlstm_scan_fwd",
    )(kc, vtc, grc, qtx, kx, vtx, acx, grx)


def _ml_readout_kernel(hf_ref, hb_ref, og_ref, x_ref, mod_ref, nw_ref, mnw_ref, wout_ref, w1_ref, w2_ref,
                       o_ref, z_ref, h_ref, x1_ref, acc_ref):
    mod = mod_ref[0]
    nw = nw_ref[...]
    mnw = mnw_ref[...]
    parts = _row_parts(x_ref.shape[0], 128)
    for r in parts:
        for hd in range(ML_HEADS):
            vs = slice(hd * ML_DV, (hd + 1) * ML_DV)
            hs = hf_ref[vs, r].astype(f32) + hb_ref[vs, r].astype(f32)
            hn = hs * lax.rsqrt(jnp.mean(hs * hs, axis=0, keepdims=True) + EPS)
            z_ref[r, vs] = (jax.nn.sigmoid(og_ref[r, vs].astype(f32)) * (hn.T * mnw[:, vs])).astype(bf16)
    for r in parts:
        _mixer_residual(x_ref[r, :], _dot(z_ref[r, :], wout_ref[...]), mod, nw, x1_ref, h_ref, r)
    _mlp_matmuls(h_ref, acc_ref, w1_ref, w2_ref, parts)
    for r in parts:
        _mlp_out(x1_ref, acc_ref, o_ref, mod, nw, r)


def _ml_readout_mlp(hf_t, hb_t, og, x2, mod, nw, ml_norm_w, w_out, w1, w2, *, layer, tokens_per_mod):
    n, d = x2.shape
    tm = TOKEN_TILE
    nv = ML_HEADS * ML_DV
    assert n % tm == 0 and tokens_per_mod % tm == 0
    tok = lambda width: pl.BlockSpec((tm, width), lambda i: (i, 0))
    tok_t = lambda width: pl.BlockSpec((width, tm), lambda i: (0, i))
    return pl.pallas_call(
        _ml_readout_kernel,
        grid=(n // tm,),
        in_specs=[tok_t(nv), tok_t(nv), tok(nv), tok(d),
                  pl.BlockSpec((1, 6, d), lambda i: (i * tm // tokens_per_mod, 0, 0)),
                  _resident(nw.shape), _resident(ml_norm_w.shape), _resident(w_out.shape),
                  _resident_layer(w1.shape, layer), _resident_layer(w2.shape, layer)],
        out_specs=tok(d),
        out_shape=jax.ShapeDtypeStruct((n, d), f32),
        scratch_shapes=[pltpu.VMEM((tm, nv), bf16), pltpu.VMEM((tm, d), bf16),
                        pltpu.VMEM((tm, d), f32), pltpu.VMEM((tm, d), f32)],
        compiler_params=_params("parallel"),
        name="mlstm_readout_mlp",
    )(hf_t, hb_t, og, x2, mod, nw, ml_norm_w, w_out, w1, w2)


def kernel(x, c, ctx, c_ctx, norm_w, mod_w, mod_b, mlp_w1, mlp_w2, conv_w_in, conv_w, conv_w_out,
           ml_w_qkvo, ml_w_if, ml_b_if, ml_norm_w, ml_w_out):
    batch, seq, d = x.shape
    t_ctx = ctx.shape[1]
    depth = norm_w.shape[0]
    assert depth == 2 and seq % GRID_W == 0 and d == ML_HEADS * ML_DV

    pad_rows = -(batch + 1) % 8
    cc = jnp.concatenate([c, c_ctx[None, :], jnp.zeros((pad_rows, d), f32)], axis=0)
    mod = _adaln(cc, mod_w, mod_b).reshape(depth, batch + 1 + pad_rows, 6, d)

    xs = x.reshape(batch * seq, d)
    cs = ctx.reshape(batch * t_ctx, d)

    mx, mc = mod[0, :batch], mod[0, batch:batch + 1]
    w_in = conv_w_in[0].astype(bf16)
    w_out = conv_w_out[0].astype(bf16)
    w1 = mlp_w1.astype(bf16)
    w2 = mlp_w2.astype(bf16)
    xs = _conv_mixer_mlp(xs, mx, norm_w[0], w_in, conv_w[0], w_out, w1, w2, layer=0, row_len=GRID_W,
                         tokens_per_mod=seq)
    cs = _conv_mixer_mlp(cs, mc, norm_w[0], w_in, conv_w[0], w_out, w1, w2, layer=0, row_len=t_ctx,
                         tokens_per_mod=batch * t_ctx)

    mx, mc = mod[1, :batch], mod[1, batch:batch + 1]
    nh = ML_HEADS
    nqk = nh * ML_DQK
    wq, wk, wv, wo = (ml_w_qkvo[0][:, lo:hi] for lo, hi in
                      ((0, nqk), (nqk, 2 * nqk), (2 * nqk, 2 * nqk + d), (2 * nqk + d, 2 * nqk + 2 * d)))
    w_gate = jnp.concatenate([ml_w_if[0, 0, :, :nh], ml_w_if[0, 1, :, :nh],
                              ml_w_if[0, 0, :, nh:], ml_w_if[0, 1, :, nh:]], axis=1)
    b_gate = jnp.concatenate([ml_b_if[0, 0, :nh], ml_b_if[0, 1, :nh],
                              ml_b_if[0, 0, nh:], ml_b_if[0, 1, nh:]])[:, None]
    w_proj_t = jnp.concatenate([wq, w_gate, wv], axis=1).T.astype(bf16)
    w_proj = jnp.concatenate([wk, wo], axis=1).astype(bf16)
    qtx, kx, vtx, ogx, acx, grx = _ml_proj(xs, mx, norm_w[1], w_proj_t, w_proj, b_gate, tokens_per_mod=seq)
    _, kc, vtc, _, _, grc = _ml_proj(cs, mc, norm_w[1], w_proj_t, w_proj, b_gate, tokens_per_mod=batch * t_ctx)
    hf_t = _ml_scan((kc, vtc, grc), (qtx, kx, vtx, acx, grx), batch=batch, reverse=False)
    hb_t = _ml_scan((kc, vtc, grc), (qtx, kx, vtx, acx, grx), batch=batch, reverse=True)
    xs = _ml_readout_mlp(hf_t, hb_t, ogx, xs, mx, norm_w[1], ml_norm_w[0][None, :], ml_w_out[0].astype(bf16),
                         w1, w2, layer=1, tokens_per_mod=seq)
    return xs.reshape(batch, seq, d)
```
